```python
import math
import jax, jax.numpy as jnp
from jax import lax
import numpy as np

D_MODEL = 1024
BATCH = 8
SEQ = 8192
DEPTH = 1

PLE_DIM = 256
W_CONV = D_MODEL
CONV_GROUPS = 8
CONV_K = 31
HG_HEAD_DIM = 128
W_HGRN = D_MODEL
HG_HEADS = W_HGRN // HG_HEAD_DIM
W_MIX = W_CONV + W_HGRN
CHUNK = 64
EPS = 1e-6
W_IN_COLS = 3 * W_CONV + 4 * W_HGRN

kernel_name = "hymba_conformer_hgrn2_hybrid"


def rms_norm(x, g):
    xf = x.astype(jnp.float32)
    y = xf * lax.rsqrt(jnp.mean(xf * xf, axis=-1, keepdims=True) + EPS)
    return (y * g.astype(jnp.float32)).astype(x.dtype)


def group_layer_norm(x, g, b, n_groups):
    shp = x.shape
    xf = x.astype(jnp.float32).reshape(shp[:-1] + (n_groups, shp[-1] // n_groups))
    mu = jnp.mean(xf, axis=-1, keepdims=True)
    var = jnp.mean(jnp.square(xf - mu), axis=-1, keepdims=True)
    y = ((xf - mu) * lax.rsqrt(var + EPS)).reshape(shp)
    return (y * g.astype(jnp.float32) + b.astype(jnp.float32)).astype(x.dtype)


def conformer_conv_branch(z_val, z_glu, z_gate, conv_w, conv_b, cn_g, cn_b, w_pw2, b_pw2):
    v = z_val * jax.nn.sigmoid(z_glu)
    y = lax.conv_general_dilated(
        v, conv_w[:, None, :].astype(v.dtype),
        window_strides=(1,), padding=[(CONV_K - 1, 0)],
        dimension_numbers=("NWC", "WIO", "NWC"),
        feature_group_count=W_CONV) + conv_b.astype(v.dtype)
    y = jax.nn.silu(group_layer_norm(y, cn_g, cn_b, CONV_GROUPS))
    y = y @ w_pw2.astype(y.dtype) + b_pw2.astype(y.dtype)
    return y * jax.nn.silu(z_gate)


def _gla_chunk_step(S, inp):
    q, k, v, lf = inp
    b = jnp.cumsum(lf, axis=2)
    o_inter = jnp.einsum("bhck,bhkv->bhcv", q * jnp.exp(b), S)
    t_idx = jnp.arange(CHUNK)
    causal = (t_idx[:, None] >= t_idx[None, :])[None, None, :, :, None]
    diff = b[:, :, :, None, :] - b[:, :, None, :, :]
    decay = jnp.exp(jnp.where(causal, diff, -jnp.inf))
    A = jnp.einsum("bhtsk,bhsk->bhts", q[:, :, :, None, :] * decay, k)
    o_intra = jnp.einsum("bhts,bhsv->bhtv", A, v)
    b_last = b[:, :, -1, :]
    k_dec = k * jnp.exp(b_last[:, :, None, :] - b)
    S_new = jnp.exp(b_last)[..., None] * S + jnp.einsum("bhsk,bhsv->bhkv", k_dec, v)
    return S_new, o_inter + o_intra


def hgrn2_branch(zq, zf, zi, zg, lb, onorm_g):
    B, T, _ = zq.shape
    n_chunks = T // CHUNK
    lbf = lb.astype(jnp.float32)
    zf32 = zf.astype(jnp.float32)
    f = lbf + (1.0 - lbf) * jax.nn.sigmoid(zf32)
    log_f = jnp.log(f)
    k = (1.0 - lbf) * jax.nn.sigmoid(-zf32)
    q = jax.nn.silu(zq.astype(jnp.float32))
    v = zi.astype(jnp.float32)

    def to_chunks(t):
        return t.reshape(B, n_chunks, CHUNK, HG_HEADS, HG_HEAD_DIM).transpose(1, 0, 3, 2, 4)

    S0 = jnp.zeros((B, HG_HEADS, HG_HEAD_DIM, HG_HEAD_DIM), jnp.float32)
    _, o = lax.scan(_gla_chunk_step, S0,
                    (to_chunks(q), to_chunks(k), to_chunks(v), to_chunks(log_f)))
    o = o.transpose(1, 0, 3, 2, 4).reshape(B, T, HG_HEADS, HG_HEAD_DIM)
    o = rms_norm(o, onorm_g.reshape(HG_HEADS, HG_HEAD_DIM))
    o = o.reshape(B, T, W_HGRN).astype(zq.dtype)
    return o * jax.nn.silu(zg)


def setup_inputs(seed: int = 0) -> dict:
    key = jax.random.key(seed)
    ks = jax.random.split(key, 20)
    f32 = jnp.float32
    nrm = lambda k, s, sc: (jax.random.normal(k, s, f32) * sc).astype(f32)
    return {
        "x": jax.random.normal(ks[0], (BATCH, SEQ, D_MODEL), f32),
        "p": jax.random.normal(ks[1], (DEPTH, BATCH, SEQ, PLE_DIM), f32),
        "ln_g": 1.0 + nrm(ks[2], (DEPTH, D_MODEL), 0.02),
        "w_in": nrm(ks[3], (DEPTH, D_MODEL, W_IN_COLS), D_MODEL ** -0.5),
        "conv_w": nrm(ks[4], (DEPTH, CONV_K, W_CONV), CONV_K ** -0.5),
        "conv_b": nrm(ks[5], (DEPTH, W_CONV), 0.01),
        "cnorm_g": 1.0 + nrm(ks[6], (DEPTH, W_CONV), 0.02),
        "cnorm_b": nrm(ks[7], (DEPTH, W_CONV), 0.01),
        "w_pw2": nrm(ks[8], (DEPTH, W_CONV, W_CONV), W_CONV ** -0.5),
        "b_pw2": nrm(ks[9], (DEPTH, W_CONV), 0.01),
        "lb_logits": nrm(ks[10], (DEPTH + 1, W_HGRN), 0.1),
        "onorm_g": 1.0 + nrm(ks[11], (DEPTH, W_HGRN), 0.02),
        "w_out": nrm(ks[12], (DEPTH, W_MIX, D_MODEL), W_MIX ** -0.5),
        "pe_norm_g": 1.0 + nrm(ks[13], (DEPTH, D_MODEL), 0.02),
        "w_pg": nrm(ks[14], (DEPTH, D_MODEL, D_MODEL), D_MODEL ** -0.5),
        "w_pp": nrm(ks[15], (DEPTH, PLE_DIM, D_MODEL), PLE_DIM ** -0.5),
        "final_g": 1.0 + nrm(ks[16], (D_MODEL,), 0.02),
    }


def reference(x, p, ln_g, w_in, conv_w, conv_b, cnorm_g, cnorm_b, w_pw2, b_pw2,
              lb_logits, onorm_g, w_out, pe_norm_g, w_pg, w_pp, final_g):
    lbs = jnp.cumsum(jax.nn.softmax(lb_logits.astype(jnp.float32), axis=0), axis=0)
    h = x
    split_pts = [W_CONV, 2 * W_CONV, 3 * W_CONV,
                 3 * W_CONV + W_HGRN, 3 * W_CONV + 2 * W_HGRN, 3 * W_CONV + 3 * W_HGRN]
    for i in range(DEPTH):
        u = rms_norm(h, ln_g[i])
        z = u @ w_in[i].astype(u.dtype)
        c_val, c_glu, c_gate, hq, hf, hi, hg = jnp.split(z, split_pts, axis=-1)
        y_conv = conformer_conv_branch(c_val, c_glu, c_gate, conv_w[i], conv_b[i],
                                       cnorm_g[i], cnorm_b[i], w_pw2[i], b_pw2[i])
        y_hgrn = hgrn2_branch(hq, hf, hi, hg, lbs[i], onorm_g[i])
        y = jnp.concatenate([y_conv, y_hgrn.astype(y_conv.dtype)], axis=-1)
        h = h + y @ w_out[i].astype(y.dtype)
        pe = p[i] @ w_pp[i].astype(p.dtype)
        gate = jax.nn.sigmoid(rms_norm(h, pe_norm_g[i]) @ w_pg[i].astype(h.dtype))
        h = h + gate * pe.astype(h.dtype)
    return rms_norm(h, final_g)
```

```python
import functools

import numpy as np
import jax
import jax.numpy as jnp
from jax import lax
from jax.experimental import pallas as pl
from jax.experimental.pallas import tpu as pltpu

EPS = 1e-6
CONV_K = 31
GROUP = 128
CHUNK = 64
LOG2_CHUNK = 6
SUBLANES = 8
HALO = 32
TIME_TILE = 512
VMEM_LIMIT_BYTES = 56 * 1024 * 1024

BF16 = jnp.bfloat16
F32 = jnp.float32


def _rms_scale(x):
    return x * lax.rsqrt(jnp.mean(x * x, axis=-1, keepdims=True) + EPS)


def _sigmoid(x):
    return jax.nn.sigmoid(x)


def _dot(a, b):
    return jnp.dot(a, b, preferred_element_type=F32)


def _dot_nt(a, b):
    return lax.dot_general(a, b, (((1,), (1,)), ((), ())), preferred_element_type=F32)


def _dot_tn(a, b):
    return lax.dot_general(a, b, (((0,), (0,)), ((), ())), preferred_element_type=F32)


def _conv_branch_kernel(x_ref, ln_g_ref, w_in_ref, conv_w_ref, conv_b_ref, cn_g_ref, cn_b_ref,
                        w_pw2_ref, b_pw2_ref, y_ref, vbuf_ref, conv_ref, wb_ref, *, tile, width):
    t_idx = pl.program_id(1)

    u = (_rms_scale(x_ref[...]) * ln_g_ref[...]).astype(BF16)
    z = _dot(u, w_in_ref[...])
    v = z[:, :width] * _sigmoid(z[:, width:2 * width])

    @pl.when(t_idx == 0)
    def _():
        vbuf_ref[0:HALO, :] = jnp.zeros((HALO, width), F32)

    vbuf_ref[HALO:HALO + tile, :] = v

    @pl.when((pl.program_id(0) == 0) & (t_idx == 0))
    def _():
        for j in range(CONV_K):
            wb_ref[SUBLANES * j:SUBLANES * (j + 1), :] = jnp.broadcast_to(
                conv_w_ref[j:j + 1, :], (SUBLANES, width))

    sub = lax.broadcasted_iota(jnp.int32, (SUBLANES, GROUP), 0)
    for lt in range(width // GROUP):
        lanes = slice(lt * GROUP, (lt + 1) * GROUP)
        bias = jnp.broadcast_to(conv_b_ref[:, lanes], (SUBLANES, GROUP))

        def partial_sums(row):
            blocks = [vbuf_ref[pl.ds(HALO + row - SUBLANES * a, SUBLANES), lanes]
                      for a in range(HALO // SUBLANES)]
            sums = []
            for r in range(SUBLANES):
                g = None
                for a, blk in enumerate(blocks):
                    s = SUBLANES * a + r
                    if s < CONV_K:
                        j = CONV_K - 1 - s
                        term = blk * wb_ref[SUBLANES * j:SUBLANES * (j + 1), lanes]
                        g = term if g is None else g + term
                sums.append(g)
            return sums

        def conv_rows(i, prev):
            r0 = pl.multiple_of(i * SUBLANES, SUBLANES)
            cur = partial_sums(r0)
            y = cur[0] + bias
            for r in range(1, SUBLANES):
                y = y + pltpu.roll(jnp.where(sub < SUBLANES - r, cur[r], prev[r - 1]), r, axis=0)
            conv_ref[pl.ds(r0, SUBLANES), lanes] = y
            return tuple(cur[1:])

        lax.fori_loop(0, tile // SUBLANES, conv_rows, tuple(partial_sums(-SUBLANES)[1:]))

    vbuf_ref[0:HALO, :] = vbuf_ref[tile:tile + HALO, :]

    for g in range(width // GROUP):
        sl = slice(g * GROUP, (g + 1) * GROUP)
        yg = conv_ref[:, sl]
        d = yg - jnp.mean(yg, axis=-1, keepdims=True)
        yn = d * lax.rsqrt(jnp.mean(d * d, axis=-1, keepdims=True) + EPS)
        yn = yn * cn_g_ref[:, sl] + cn_b_ref[:, sl]
        conv_ref[:, sl] = yn * _sigmoid(yn)

    y = _dot(conv_ref[...].astype(BF16), w_pw2_ref[...]) + b_pw2_ref[...]
    z_gate = z[:, 2 * width:]
    y_ref[...] = (y * (z_gate * _sigmoid(z_gate))).astype(y_ref.dtype)


def _decay_sum_matrix():
    m = np.zeros((LOG2_CHUNK + 1, CHUNK, CHUNK), np.float32)
    r = np.arange(CHUNK)
    for t in range(CHUNK):
        m[0, t] = r <= t
        m[1, t] = r > t
        for j in range(1, LOG2_CHUNK):
            half = 1 << j
            start = (t >> (j + 1)) << (j + 1)
            if (t >> j) & 1:
                m[1 + j, t] = (r >= start + half) & (r <= t)
            else:
                m[1 + j, t] = (r > t) & (r <= start + half - 1)
    return m.reshape((LOG2_CHUNK + 1) * CHUNK, CHUNK)


def _hgrn_branch_kernel(x_ref, ln_g_ref, w_in_ref, lb_logits_ref, onorm_g_ref, dsum_ref, y_ref,
                        q_ref, k_ref, v_ref, lf_ref, o_ref, state_ref, *, tile, width, layer):
    t_idx = pl.program_id(1)
    heads = width // GROUP

    @pl.when(t_idx == 0)
    def _():
        state_ref[...] = jnp.zeros(state_ref.shape, F32)

    u = (_rms_scale(x_ref[...]) * ln_g_ref[...]).astype(BF16)
    z = _dot(u, w_in_ref[...])

    logits = lb_logits_ref[...]
    e = jnp.exp(logits - jnp.max(logits, axis=0, keepdims=True))
    lb = jnp.sum(e[:layer + 1], axis=0, keepdims=True) / jnp.sum(e, axis=0, keepdims=True)

    zq = z[:, :width]
    sig_f = _sigmoid(z[:, width:2 * width])
    q_ref[...] = zq * _sigmoid(zq)
    lf_ref[...] = jnp.log(lb + (1.0 - lb) * sig_f)
    k_ref[...] = (1.0 - lb) * (1.0 - sig_f)
    v_ref[...] = z[:, 2 * width:3 * width]

    row = lax.broadcasted_iota(jnp.int32, (CHUNK, CHUNK), 0)
    col = lax.broadcasted_iota(jnp.int32, (CHUNK, CHUNK), 1)
    row_xor_col = row ^ col
    causal = row > col
    level_masks = [causal & ((row_xor_col >> j) == 1) for j in range(LOG2_CHUNK)]
    diag_mask = row == col
    odd_row = (lax.broadcasted_iota(jnp.int32, (CHUNK, width), 0) & 1) == 1

    def chunk_step(c, carry):
        r0 = pl.multiple_of(c * CHUNK, CHUNK)
        lf = lf_ref[pl.ds(r0, CHUNK), :]
        lf_hi = lf.astype(BF16)
        lf_lo = (lf - lf_hi.astype(F32)).astype(BF16)
        dsum = dsum_ref[...]
        decay = jnp.exp(_dot(dsum, lf_hi) + _dot(dsum, lf_lo))
        q = q_ref[pl.ds(r0, CHUNK), :]
        k = k_ref[pl.ds(r0, CHUNK), :]
        v = v_ref[pl.ds(r0, CHUNK), :].astype(BF16)
        e_b = decay[0:CHUNK]
        e_rest = decay[CHUNK:2 * CHUNK]
        w_levels = [jnp.where(odd_row, jnp.exp(lf), 1.0)]
        for j in range(1, LOG2_CHUNK):
            w_levels.append(decay[(1 + j) * CHUNK:(2 + j) * CHUNK])

        for h in range(heads):
            sl = slice(h * GROUP, (h + 1) * GROUP)
            qh, kh, vh = q[:, sl], k[:, sl], v[:, sl]
            s_t = state_ref[h]
            o = _dot_nt((qh * e_b[:, sl]).astype(BF16), s_t.astype(BF16))
            a = jnp.where(diag_mask, _dot_nt(qh.astype(BF16), kh.astype(BF16)), 0.0)
            for j in range(LOG2_CHUNK):
                wj = w_levels[j][:, sl]
                p = _dot_nt((qh * wj).astype(BF16), (kh * wj).astype(BF16))
                a = jnp.where(level_masks[j], p, a)
            o = o + _dot(a.astype(BF16), vh)
            o_ref[pl.ds(r0, CHUNK), sl] = o
            k_dec = (kh * e_rest[:, sl]).astype(BF16)
            state_ref[h] = e_b[CHUNK - 1:CHUNK, sl] * s_t + _dot_tn(vh, k_dec)
        return carry

    lax.fori_loop(0, tile // CHUNK, chunk_step, 0)

    zg = z[:, 3 * width:]
    gate = zg * _sigmoid(zg)
    for h in range(heads):
        sl = slice(h * GROUP, (h + 1) * GROUP)
        y_ref[:, sl] = (_rms_scale(o_ref[:, sl]) * onorm_g_ref[:, sl] * gate[:, sl]).astype(y_ref.dtype)


def _output_kernel(x_ref, yc_ref, yh_ref, p_ref, w_out_c_ref, w_out_h_ref, pe_g_ref, w_pg_ref,
                   w_pp_ref, final_g_ref, out_ref):
    h = x_ref[...] + _dot(yc_ref[...], w_out_c_ref[...]) + _dot(yh_ref[...], w_out_h_ref[...])
    pe = _dot(p_ref[...].astype(BF16), w_pp_ref[...])
    gate = _sigmoid(_dot((_rms_scale(h) * pe_g_ref[...]).astype(BF16), w_pg_ref[...]))
    h = h + gate * pe
    out_ref[...] = _rms_scale(h) * final_g_ref[...]


def _full(shape):
    return pl.BlockSpec(shape, lambda b, t: (0,) * len(shape))


def _layer(x, p_i, ln_g, w_in, conv_w, conv_b, cn_g, cn_b, w_pw2, b_pw2, lb_logits, onorm_g,
           w_out, pe_g, w_pg, w_pp, out_g, layer):
    batch, seq, d_model = x.shape
    width = conv_w.shape[-1]
    ple = p_i.shape[-1]
    tile = min(TIME_TILE, seq)
    assert seq % tile == 0 and tile % CHUNK == 0 and width % GROUP == 0
    grid = (batch, seq // tile)
    row2 = lambda a: a.reshape(1, -1)
    tok = lambda w: pl.BlockSpec((None, tile, w), lambda b, t: (b, t, 0))
    params = pltpu.CompilerParams(dimension_semantics=("arbitrary", "arbitrary"),
                                  vmem_limit_bytes=VMEM_LIMIT_BYTES)

    w_in_c = w_in[:, :3 * width].astype(BF16)
    w_in_h = w_in[:, 3 * width:].astype(BF16)

    y_conv = pl.pallas_call(
        functools.partial(_conv_branch_kernel, tile=tile, width=width),
        grid=grid,
        in_specs=[tok(d_model), _full((1, d_model)), _full((d_model, 3 * width)),
                  _full((CONV_K, width)), _full((1, width)), _full((1, width)), _full((1, width)),
                  _full((width, width)), _full((1, width))],
        out_specs=tok(width),
        out_shape=jax.ShapeDtypeStruct((batch, seq, width), BF16),
        scratch_shapes=[pltpu.VMEM((HALO + tile, width), F32), pltpu.VMEM((tile, width), F32),
                        pltpu.VMEM((SUBLANES * CONV_K, width), F32)],
        compiler_params=params,
        name="conv_branch",
    )(x, row2(ln_g), w_in_c, conv_w, row2(conv_b), row2(cn_g), row2(cn_b),
      w_pw2.astype(BF16), row2(b_pw2))

    dsum = jnp.asarray(_decay_sum_matrix(), BF16)
    y_hgrn = pl.pallas_call(
        functools.partial(_hgrn_branch_kernel, tile=tile, width=width, layer=layer),
        grid=grid,
        in_specs=[tok(d_model), _full((1, d_model)), _full((d_model, 4 * width)),
                  _full(lb_logits.shape), _full((1, width)), _full(dsum.shape)],
        out_specs=tok(width),
        out_shape=jax.ShapeDtypeStruct((batch, seq, width), BF16),
        scratch_shapes=[pltpu.VMEM((tile, width), F32)] * 5
                       + [pltpu.VMEM((width // GROUP, GROUP, GROUP), F32)],
        compiler_params=params,
        name="hgrn_branch",
    )(x, row2(ln_g), w_in_h, lb_logits, row2(onorm_g), dsum)

    return pl.pallas_call(
        _output_kernel,
        grid=grid,
        in_specs=[tok(d_model), tok(width), tok(width), tok(ple),
                  _full((width, d_model)), _full((width, d_model)), _full((1, d_model)),
                  _full((d_model, d_model)), _full((ple, d_model)), _full((1, d_model))],
        out_specs=tok(d_model),
        out_shape=jax.ShapeDtypeStruct((batch, seq, d_model), x.dtype),
        compiler_params=params,
        name="output_proj",
    )(x, y_conv, y_hgrn, p_i, w_out[:width].astype(BF16), w_out[width:].astype(BF16),
      row2(pe_g), w_pg.astype(BF16), w_pp.astype(BF16), row2(out_g))


def kernel(x, p, ln_g, w_in, conv_w, conv_b, cnorm_g, cnorm_b, w_pw2, b_pw2, lb_logits, onorm_g,
           w_out, pe_norm_g, w_pg, w_pp, final_g):
    depth = p.shape[0]
    assert depth == 1, "the final norm is fused into the only layer's output kernel"
    return _layer(x, p[0], ln_g[0], w_in[0], conv_w[0], conv_b[0], cnorm_g[0], cnorm_b[0],
                  w_pw2[0], b_pw2[0], lb_logits, onorm_g[0], w_out[0], pe_norm_g[0], w_pg[0],
                  w_pp[0], final_g, layer=0)
```

```python
import functools

import numpy as np
import jax
import jax.numpy as jnp
from jax import lax
from jax.experimental import pallas as pl
from jax.experimental.pallas import tpu as pltpu

EPS = 1e-6
CONV_K = 31
GROUP = 128
CHUNK = 64
LOG2_CHUNK = 6
SUBLANES = 8
HALO = 32
CONV_LANES = 256
CONV_UNROLL = 4
TIME_TILE = 512
VMEM_LIMIT_BYTES = 56 * 1024 * 1024

HIGH_HALF = np.uint32(0xFFFF0000)

BF16 = jnp.bfloat16
F32 = jnp.float32


def _rms_scale(x):
    return x * lax.rsqrt(jnp.mean(x * x, axis=-1, keepdims=True) + EPS)


def _sigmoid(x):
    return jax.nn.sigmoid(x)


def _dot(a, b):
    return jnp.dot(a, b, preferred_element_type=F32)


def _dot_nt(a, b):
    return lax.dot_general(a, b, (((1,), (1,)), ((), ())), preferred_element_type=F32)


def _dot_tn(a, b):
    return lax.dot_general(a, b, (((0,), (0,)), ((), ())), preferred_element_type=F32)


def _conv_branch_kernel(x_ref, ln_g_ref, w_in_ref, conv_w_ref, conv_b_ref, cn_g_ref, cn_b_ref,
                        w_pw2_ref, b_pw2_ref, y_ref, vbuf_ref, conv_ref, wb_ref, *, tile, width):
    t_idx = pl.program_id(1)

    u = (_rms_scale(x_ref[...]) * ln_g_ref[...]).astype(BF16)
    z = _dot(u, w_in_ref[...])
    v = z[:, :width] * _sigmoid(z[:, width:2 * width])

    @pl.when(t_idx == 0)
    def _():
        vbuf_ref[0:HALO, :] = jnp.zeros((HALO, width), F32)

    vbuf_ref[HALO:HALO + tile, :] = v

    @pl.when((pl.program_id(0) == 0) & (t_idx == 0))
    def _():
        for j in range(CONV_K):
            wb_ref[SUBLANES * j:SUBLANES * (j + 1), :] = jnp.broadcast_to(
                conv_w_ref[j:j + 1, :], (SUBLANES, width))

    sub = lax.broadcasted_iota(jnp.int32, (SUBLANES, CONV_LANES), 0)
    for lt in range(width // CONV_LANES):
        lanes = slice(lt * CONV_LANES, (lt + 1) * CONV_LANES)
        bias = jnp.broadcast_to(conv_b_ref[:, lanes], (SUBLANES, CONV_LANES))

        def partial_sums(row):
            blocks = [vbuf_ref[pl.ds(HALO + row - SUBLANES * a, SUBLANES), lanes]
                      for a in range(HALO // SUBLANES)]
            sums = []
            for r in range(SUBLANES):
                g = None
                for a, blk in enumerate(blocks):
                    s = SUBLANES * a + r
                    if s < CONV_K:
                        j = CONV_K - 1 - s
                        term = blk * wb_ref[SUBLANES * j:SUBLANES * (j + 1), lanes]
                        g = term if g is None else g + term
                sums.append(g)
            return sums

        def conv_rows(i, prev):
            r0 = pl.multiple_of(i * SUBLANES, SUBLANES)
            cur = partial_sums(r0)
            y = cur[0] + bias
            for r in range(1, SUBLANES):
                y = y + pltpu.roll(jnp.where(sub < SUBLANES - r, cur[r], prev[r - 1]), r, axis=0)
            conv_ref[pl.ds(r0, SUBLANES), lanes] = y
            return tuple(cur[1:])

        lax.fori_loop(0, tile // SUBLANES, conv_rows, tuple(partial_sums(-SUBLANES)[1:]),
                      unroll=CONV_UNROLL)

    vbuf_ref[0:HALO, :] = vbuf_ref[tile:tile + HALO, :]

    for g in range(width // GROUP):
        sl = slice(g * GROUP, (g + 1) * GROUP)
        yg = conv_ref[:, sl]
        d = yg - jnp.mean(yg, axis=-1, keepdims=True)
        yn = d * lax.rsqrt(jnp.mean(d * d, axis=-1, keepdims=True) + EPS)
        yn = yn * cn_g_ref[:, sl] + cn_b_ref[:, sl]
        conv_ref[:, sl] = yn * _sigmoid(yn)

    y = _dot(conv_ref[...].astype(BF16), w_pw2_ref[...]) + b_pw2_ref[...]
    z_gate = z[:, 2 * width:]
    y_ref[...] = (y * (z_gate * _sigmoid(z_gate))).astype(y_ref.dtype)


def _boundary_rows(b, j):
    half = 1 << j
    rows, width = b.shape
    if 2 * half >= SUBLANES:
        return jnp.concatenate(
            [jnp.broadcast_to(b[s + half - 1:s + half, :], (2 * half, width))
             for s in range(0, rows, 2 * half)], axis=0)
    assert 4 * half == SUBLANES
    sub = lax.broadcasted_iota(jnp.int32, (SUBLANES, width), 0)
    return jnp.concatenate(
        [jnp.where(sub < 2 * half,
                   jnp.broadcast_to(b[s + half - 1:s + half, :], (SUBLANES, width)),
                   jnp.broadcast_to(b[s + 3 * half - 1:s + 3 * half, :], (SUBLANES, width)))
         for s in range(0, rows, SUBLANES)], axis=0)


def _block_diag_pair(m):
    zero = jnp.zeros((m.shape[0], GROUP), m.dtype)
    return jnp.concatenate([jnp.concatenate([m[:, :GROUP], zero], axis=1),
                            jnp.concatenate([zero, m[:, GROUP:]], axis=1)], axis=0)


def _hgrn_branch_kernel(x_ref, ln_g_ref, w_in_ref, lb_logits_ref, onorm_g_ref, tri_ref, y_ref,
                        state_ref, *, tile, width, layer):
    heads = width // GROUP
    n_chunks = tile // CHUNK

    @pl.when(pl.program_id(1) == 0)
    def _():
        state_ref[...] = jnp.zeros(state_ref.shape, F32)

    u = (_rms_scale(x_ref[...]) * ln_g_ref[...]).astype(BF16)
    z = _dot(u, w_in_ref[...])

    logits = lb_logits_ref[...]
    e = jnp.exp(logits - jnp.max(logits, axis=0, keepdims=True))
    lb = jnp.sum(e[:layer + 1], axis=0, keepdims=True) / jnp.sum(e, axis=0, keepdims=True)

    row = lax.broadcasted_iota(jnp.int32, (CHUNK, 2 * CHUNK), 0)
    col = lax.broadcasted_iota(jnp.int32, (CHUNK, 2 * CHUNK), 1) & (CHUNK - 1)
    row_xor_col = row ^ col
    causal = row > col
    level_masks = [causal & ((row_xor_col >> j) == 1) for j in range(LOG2_CHUNK)]
    diag_mask = row == col
    odd_row = (lax.broadcasted_iota(jnp.int32, (CHUNK, width), 0) & 1) == 1
    tri = tri_ref[...]

    def gates(c):
        rows = slice(c * CHUNK, (c + 1) * CHUNK)
        zq = z[rows, :width]
        sig_f = _sigmoid(z[rows, width:2 * width])
        zg = z[rows, 3 * width:]
        lf = jnp.log(lb + (1.0 - lb) * sig_f)
        lf_top = lax.bitcast_convert_type(lax.bitcast_convert_type(lf, jnp.uint32) & HIGH_HALF, F32)
        return dict(q=zq * _sigmoid(zq),
                    k=(1.0 - lb) * (1.0 - sig_f),
                    lf=lf, lf_top=lf_top.astype(BF16), lf_rest=(lf - lf_top).astype(BF16),
                    v=z[rows, 2 * width:3 * width].astype(BF16), gate=zg * _sigmoid(zg))

    def cumulative_decay(g):
        return _dot(tri, g["lf_top"]) + _dot(tri, g["lf_rest"])

    def scaled_operands(g, b):
        q, k, lf = g["q"], g["k"], g["lf"]
        e_b = jnp.exp(b)
        b_last = b[CHUNK - 1:CHUNK, :]
        factors = [jnp.where(odd_row, jnp.exp(lf), 1.0)]
        for j in range(1, LOG2_CHUNK):
            factors.append(jnp.exp(-jnp.abs(b - _boundary_rows(b, j))))
        q16, k16 = q.astype(BF16), k.astype(BF16)
        q_lvl, k_lvl = [q16], [k16]
        for w in factors:
            w16 = w.astype(BF16)
            q_lvl.append(q16 * w16)
            k_lvl.append(k16 * w16)
        return dict(q_lvl=q_lvl, k_lvl=k_lvl, q_dec=(q * e_b).astype(BF16),
                    k_dec=(k * jnp.exp(b_last - b)).astype(BF16), e_last=jnp.exp(b_last),
                    v=g["v"], gate=g["gate"])

    def recur(c, ops, state, mid_mxu_work):
        rows = slice(c * CHUNK, (c + 1) * CHUNK)
        intra = []
        for p in range(heads // 2):
            pair = slice(2 * p * GROUP, 2 * (p + 1) * GROUP)
            prods = [_dot_nt(ql[:, pair], _block_diag_pair(kl[:, pair]))
                     for ql, kl in zip(ops["q_lvl"], ops["k_lvl"])]
            a = jnp.where(diag_mask, prods[0], 0.0)
            for j in range(LOG2_CHUNK):
                a = jnp.where(level_masks[j], prods[1 + j], a)
            intra.append((a.astype(BF16), _block_diag_pair(ops["v"][:, pair])))
        inter, new_state = [], []
        for h in range(heads):
            sl = slice(h * GROUP, (h + 1) * GROUP)
            s_t = state[h]
            inter.append(_dot_nt(ops["q_dec"][:, sl], s_t.astype(BF16)))
            new_state.append(ops["e_last"][:, sl] * s_t + _dot_tn(ops["v"][:, sl], ops["k_dec"][:, sl]))
        mid = mid_mxu_work()
        for p, (a, v_bd) in enumerate(intra):
            o_pair = _dot(a, v_bd)
            for i in range(2):
                h = 2 * p + i
                sl = slice(h * GROUP, (h + 1) * GROUP)
                o = inter[h] + o_pair[:, i * GROUP:(i + 1) * GROUP]
                y_ref[rows, sl] = (_rms_scale(o) * onorm_g_ref[:, sl] * ops["gate"][:, sl]).astype(y_ref.dtype)
        return new_state, mid

    state = [state_ref[h] for h in range(heads)]
    g = {c: gates(c) for c in range(min(2, n_chunks))}
    b = {c: cumulative_decay(g[c]) for c in g}
    ops = scaled_operands(g[0], b[0])
    for c in range(n_chunks):
        if c + 2 < n_chunks:
            g[c + 2] = gates(c + 2)
        nxt = scaled_operands(g[c + 1], b[c + 1]) if c + 1 < n_chunks else None
        mid = (lambda c=c: cumulative_decay(g[c + 2])) if c + 2 < n_chunks else (lambda: None)
        state, b[c + 2] = recur(c, ops, state, mid)
        ops = nxt
    for h in range(heads):
        state_ref[h] = state[h]


def _output_kernel(x_ref, yc_ref, yh_ref, p_ref, w_out_c_ref, w_out_h_ref, pe_g_ref, w_pg_ref,
                   w_pp_ref, final_g_ref, out_ref):
    h = x_ref[...] + _dot(yc_ref[...], w_out_c_ref[...]) + _dot(yh_ref[...], w_out_h_ref[...])
    pe = _dot(p_ref[...].astype(BF16), w_pp_ref[...])
    gate = _sigmoid(_dot((_rms_scale(h) * pe_g_ref[...]).astype(BF16), w_pg_ref[...]))
    h = h + gate * pe
    out_ref[...] = _rms_scale(h) * final_g_ref[...]


def _full(shape):
    return pl.BlockSpec(shape, lambda b, t: (0,) * len(shape))


def _layer(x, p_i, ln_g, w_in, conv_w, conv_b, cn_g, cn_b, w_pw2, b_pw2, lb_logits, onorm_g,
           w_out, pe_g, w_pg, w_pp, out_g, layer):
    batch, seq, d_model = x.shape
    width = conv_w.shape[-1]
    ple = p_i.shape[-1]
    tile = min(TIME_TILE, seq)
    assert seq % tile == 0 and tile % CHUNK == 0 and width % (2 * GROUP) == 0
    grid = (batch, seq // tile)
    row2 = lambda a: a.reshape(1, -1)
    tok = lambda w: pl.BlockSpec((None, tile, w), lambda b, t: (b, t, 0))
    params = pltpu.CompilerParams(dimension_semantics=("arbitrary", "arbitrary"),
                                  vmem_limit_bytes=VMEM_LIMIT_BYTES)

    w_in_c = w_in[:, :3 * width].astype(BF16)
    w_in_h = w_in[:, 3 * width:].astype(BF16)

    y_conv = pl.pallas_call(
        functools.partial(_conv_branch_kernel, tile=tile, width=width),
        grid=grid,
        in_specs=[tok(d_model), _full((1, d_model)), _full((d_model, 3 * width)),
                  _full((CONV_K, width)), _full((1, width)), _full((1, width)), _full((1, width)),
                  _full((width, width)), _full((1, width))],
        out_specs=tok(width),
        out_shape=jax.ShapeDtypeStruct((batch, seq, width), BF16),
        scratch_shapes=[pltpu.VMEM((HALO + tile, width), F32), pltpu.VMEM((tile, width), F32),
                        pltpu.VMEM((SUBLANES * CONV_K, width), F32)],
        compiler_params=params,
        name="conv_branch",
    )(x, row2(ln_g), w_in_c, conv_w, row2(conv_b), row2(cn_g), row2(cn_b),
      w_pw2.astype(BF16), row2(b_pw2))

    tri = jnp.asarray(np.tril(np.ones((CHUNK, CHUNK), np.float32)), BF16)
    y_hgrn = pl.pallas_call(
        functools.partial(_hgrn_branch_kernel, tile=tile, width=width, layer=layer),
        grid=grid,
        in_specs=[tok(d_model), _full((1, d_model)), _full((d_model, 4 * width)),
                  _full(lb_logits.shape), _full((1, width)), _full(tri.shape)],
        out_specs=tok(width),
        out_shape=jax.ShapeDtypeStruct((batch, seq, width), BF16),
        scratch_shapes=[pltpu.VMEM((width // GROUP, GROUP, GROUP), F32)],
        compiler_params=params,
        name="hgrn_branch",
    )(x, row2(ln_g), w_in_h, lb_logits, row2(onorm_g), tri)

    return pl.pallas_call(
        _output_kernel,
        grid=grid,
        in_specs=[tok(d_model), tok(width), tok(width), tok(ple),
                  _full((width, d_model)), _full((width, d_model)), _full((1, d_model)),
                  _full((d_model, d_model)), _full((ple, d_model)), _full((1, d_model))],
        out_specs=tok(d_model),
        out_shape=jax.ShapeDtypeStruct((batch, seq, d_model), x.dtype),
        compiler_params=params,
        name="output_proj",
    )(x, y_conv, y_hgrn, p_i, w_out[:width].astype(BF16), w_out[width:].astype(BF16),
      row2(pe_g), w_pg.astype(BF16), w_pp.astype(BF16), row2(out_g))


def kernel(x, p, ln_g, w_in, conv_w, conv_b, cnorm_g, cnorm_b, w_pw2, b_pw2, lb_logits, onorm_g,
           w_out, pe_norm_g, w_pg, w_pp, final_g):
    depth = p.shape[0]
    assert depth == 1, "the final norm is fused into the only layer's output kernel"
    return _layer(x, p[0], ln_g[0], w_in[0], conv_w[0], conv_b[0], cnorm_g[0], cnorm_b[0],
                  w_pw2[0], b_pw2[0], lb_logits, onorm_g[0], w_out[0], pe_norm_g[0], w_pg[0],
                  w_pp[0], final_g, layer=0)
```

```python
import functools

import numpy as np
import jax
import jax.numpy as jnp
from jax import lax
from jax.experimental import pallas as pl
from jax.experimental.pallas import tpu as pltpu

EPS = 1e-6
CONV_K = 31
GROUP = 128
CHUNK = 64
LOG2_CHUNK = 6
SUBLANES = 8
HALO = 32
CONV_LANES = 128
CONV_SLABS = 4
MIN_GATE_FOR_SINGLE_PRODUCT = 0.125
TIME_TILE = 512
VMEM_LIMIT_BYTES = 56 * 1024 * 1024

HIGH_HALF = np.uint32(0xFFFF0000)

BF16 = jnp.bfloat16
F32 = jnp.float32


def _rms_scale(x):
    return x * lax.rsqrt(jnp.mean(x * x, axis=-1, keepdims=True) + EPS)


def _sigmoid(x):
    return jax.nn.sigmoid(x)


def _dot(a, b):
    return jnp.dot(a, b, preferred_element_type=F32)


def _dot_nt(a, b):
    return lax.dot_general(a, b, (((1,), (1,)), ((), ())), preferred_element_type=F32)


def _dot_tn(a, b):
    return lax.dot_general(a, b, (((0,), (0,)), ((), ())), preferred_element_type=F32)


def _conv_branch_kernel(x_ref, ln_g_ref, w_in_ref, conv_w_ref, conv_b_ref, cn_g_ref, cn_b_ref,
                        w_pw2_ref, b_pw2_ref, y_ref, vbuf_ref, wb_ref, *, tile, width):
    t_idx = pl.program_id(1)
    slab = tile // CONV_SLABS
    slab_rows = lambda i: slice(i * slab, (i + 1) * slab)

    u = (_rms_scale(x_ref[...]) * ln_g_ref[...]).astype(BF16)

    @pl.when(t_idx == 0)
    def _():
        vbuf_ref[0:HALO, :] = jnp.zeros((HALO, width), F32)

    @pl.when((pl.program_id(0) == 0) & (t_idx == 0))
    def _():
        for j in range(CONV_K):
            wb_ref[SUBLANES * j:SUBLANES * (j + 1), :] = jnp.broadcast_to(
                conv_w_ref[j:j + 1, :], (SUBLANES, width))

    sub = lax.broadcasted_iota(jnp.int32, (SUBLANES, CONV_LANES), 0)
    lane_groups = [slice(lt * CONV_LANES, (lt + 1) * CONV_LANES) for lt in range(width // CONV_LANES)]

    def partial_sums(row, lanes):
        blocks = [vbuf_ref[HALO + row - SUBLANES * a:HALO + row - SUBLANES * (a - 1), lanes]
                  for a in range(HALO // SUBLANES)]
        sums = []
        for r in range(SUBLANES):
            g = None
            for a, blk in enumerate(blocks):
                s = SUBLANES * a + r
                if s < CONV_K:
                    j = CONV_K - 1 - s
                    term = blk * wb_ref[SUBLANES * j:SUBLANES * (j + 1), lanes]
                    g = term if g is None else g + term
            sums.append(g)
        return sums

    def project(i):
        return _dot(u[slab_rows(i)], w_in_ref[...])

    def glu(i, z):
        rows = slab_rows(i)
        vbuf_ref[HALO + rows.start:HALO + rows.stop, :] = z[:, :width] * _sigmoid(z[:, width:2 * width])

    def conv(i, prev):
        out = []
        for lt, lanes in enumerate(lane_groups):
            bias = jnp.broadcast_to(conv_b_ref[:, lanes], (SUBLANES, CONV_LANES))
            blocks_out = []
            for r0 in range(i * slab, (i + 1) * slab, SUBLANES):
                cur = partial_sums(r0, lanes)
                y = cur[0] + bias
                for r in range(1, SUBLANES):
                    y = y + pltpu.roll(jnp.where(sub < SUBLANES - r, cur[r], prev[lt][r - 1]), r, axis=0)
                blocks_out.append(y)
                prev[lt] = cur[1:]
            out.append(jnp.concatenate(blocks_out, axis=0))
        return jnp.concatenate(out, axis=1)

    def finish(i, y_conv, z):
        normed = []
        for g in range(width // GROUP):
            sl = slice(g * GROUP, (g + 1) * GROUP)
            yg = y_conv[:, sl]
            d = yg - jnp.mean(yg, axis=-1, keepdims=True)
            yn = d * lax.rsqrt(jnp.mean(d * d, axis=-1, keepdims=True) + EPS)
            yn = yn * cn_g_ref[:, sl] + cn_b_ref[:, sl]
            normed.append((yn * _sigmoid(yn)).astype(BF16))
        y = _dot(jnp.concatenate(normed, axis=1), w_pw2_ref[...]) + b_pw2_ref[...]
        z_gate = z[:, 2 * width:]
        y_ref[slab_rows(i), :] = (y * (z_gate * _sigmoid(z_gate))).astype(y_ref.dtype)

    z = {i: project(i) for i in range(min(2, CONV_SLABS))}
    glu(0, z[0])
    prev = [partial_sums(-SUBLANES, lanes)[1:] for lanes in lane_groups]
    for i in range(CONV_SLABS):
        if i + 2 < CONV_SLABS:
            z[i + 2] = project(i + 2)
        if i + 1 < CONV_SLABS:
            glu(i + 1, z[i + 1])
        finish(i, conv(i, prev), z.pop(i))

    vbuf_ref[0:HALO, :] = vbuf_ref[tile:tile + HALO, :]


def _boundary_rows(b, j):
    half = 1 << j
    rows, width = b.shape
    if 2 * half >= SUBLANES:
        return jnp.concatenate(
            [jnp.broadcast_to(b[s + half - 1:s + half, :], (2 * half, width))
             for s in range(0, rows, 2 * half)], axis=0)
    assert 4 * half == SUBLANES
    sub = lax.broadcasted_iota(jnp.int32, (SUBLANES, width), 0)
    return jnp.concatenate(
        [jnp.where(sub < 2 * half,
                   jnp.broadcast_to(b[s + half - 1:s + half, :], (SUBLANES, width)),
                   jnp.broadcast_to(b[s + 3 * half - 1:s + 3 * half, :], (SUBLANES, width)))
         for s in range(0, rows, SUBLANES)], axis=0)


def _block_diag_pair(m):
    zero = jnp.zeros((m.shape[0], GROUP), m.dtype)
    return jnp.concatenate([jnp.concatenate([m[:, :GROUP], zero], axis=1),
                            jnp.concatenate([zero, m[:, GROUP:]], axis=1)], axis=0)


def _hgrn_branch_kernel(x_ref, ln_g_ref, w_in_ref, lb_logits_ref, onorm_g_ref, tri_ref, y_ref,
                        state_ref, *, tile, width, layer):
    heads = width // GROUP
    n_chunks = tile // CHUNK

    @pl.when(pl.program_id(1) == 0)
    def _():
        state_ref[...] = jnp.zeros(state_ref.shape, F32)

    u = (_rms_scale(x_ref[...]) * ln_g_ref[...]).astype(BF16)
    z = _dot(u, w_in_ref[...])

    logits = lb_logits_ref[...]
    e = jnp.exp(logits - jnp.max(logits, axis=0, keepdims=True))
    lb = jnp.sum(e[:layer + 1], axis=0, keepdims=True) / jnp.sum(e, axis=0, keepdims=True)

    row = lax.broadcasted_iota(jnp.int32, (CHUNK, 2 * CHUNK), 0)
    col = lax.broadcasted_iota(jnp.int32, (CHUNK, 2 * CHUNK), 1) & (CHUNK - 1)
    row_xor_col = row ^ col
    causal = row > col
    level_masks = [causal & ((row_xor_col >> j) == 1) for j in range(LOG2_CHUNK)]
    diag_mask = row == col
    odd_row = (lax.broadcasted_iota(jnp.int32, (CHUNK, width), 0) & 1) == 1
    tri = tri_ref[...]

    def gates(c):
        rows = slice(c * CHUNK, (c + 1) * CHUNK)
        zq = z[rows, :width]
        sig_f = _sigmoid(z[rows, width:2 * width])
        zg = z[rows, 3 * width:]
        lf = jnp.log(lb + (1.0 - lb) * sig_f)
        lf_top = lax.bitcast_convert_type(lax.bitcast_convert_type(lf, jnp.uint32) & HIGH_HALF, F32)
        return dict(q=zq * _sigmoid(zq),
                    k=(1.0 - lb) * (1.0 - sig_f),
                    lf=lf, lf_top=lf_top.astype(BF16), lf_rest=(lf - lf_top).astype(BF16),
                    v=z[rows, 2 * width:3 * width].astype(BF16), gate=zg * _sigmoid(zg))

    def cumulative_decay(g):
        return _dot(tri, g["lf_top"]) + _dot(tri, g["lf_rest"])

    def scaled_operands_bounded(g, b):
        q, k = g["q"], g["k"]
        b_mid = b[CHUNK // 2 - 1:CHUNK // 2, :]
        b_last = b[CHUNK - 1:CHUNK, :]
        q_mid = q * jnp.exp(b - b_mid)
        k_mid = k * jnp.exp(b_mid - b)
        return dict(q_lvl=[q_mid.astype(BF16)], k_lvl=[k_mid.astype(BF16)],
                    q_dec=(q_mid * jnp.exp(b_mid)).astype(BF16),
                    k_dec=(k_mid * jnp.exp(b_last - b_mid)).astype(BF16), e_last=jnp.exp(b_last),
                    v=g["v"], gate=g["gate"])

    def scaled_operands(g, b):
        q, k, lf = g["q"], g["k"], g["lf"]
        e_b = jnp.exp(b)
        b_last = b[CHUNK - 1:CHUNK, :]
        factors = [jnp.where(odd_row, jnp.exp(lf), 1.0)]
        for j in range(1, LOG2_CHUNK):
            factors.append(jnp.exp(-jnp.abs(b - _boundary_rows(b, j))))
        q16, k16 = q.astype(BF16), k.astype(BF16)
        q_lvl, k_lvl = [q16], [k16]
        for w in factors:
            w16 = w.astype(BF16)
            q_lvl.append(q16 * w16)
            k_lvl.append(k16 * w16)
        return dict(q_lvl=q_lvl, k_lvl=k_lvl, q_dec=(q * e_b).astype(BF16),
                    k_dec=(k * jnp.exp(b_last - b)).astype(BF16), e_last=jnp.exp(b_last),
                    v=g["v"], gate=g["gate"])

    def recur(c, ops, state, mid_mxu_work):
        rows = slice(c * CHUNK, (c + 1) * CHUNK)
        intra = []
        for p in range(heads // 2):
            pair = slice(2 * p * GROUP, 2 * (p + 1) * GROUP)
            prods = [_dot_nt(ql[:, pair], _block_diag_pair(kl[:, pair]))
                     for ql, kl in zip(ops["q_lvl"], ops["k_lvl"])]
            if len(prods) == 1:
                a = jnp.where(causal | diag_mask, prods[0], 0.0)
            else:
                a = jnp.where(diag_mask, prods[0], 0.0)
                for j in range(LOG2_CHUNK):
                    a = jnp.where(level_masks[j], prods[1 + j], a)
            intra.append((a.astype(BF16), _block_diag_pair(ops["v"][:, pair])))
        inter, new_state = [], []
        for h in range(heads):
            sl = slice(h * GROUP, (h + 1) * GROUP)
            s_t = state[h]
            inter.append(_dot_nt(ops["q_dec"][:, sl], s_t.astype(BF16)))
            new_state.append(ops["e_last"][:, sl] * s_t + _dot_tn(ops["v"][:, sl], ops["k_dec"][:, sl]))
        mid = mid_mxu_work()
        for p, (a, v_bd) in enumerate(intra):
            o_pair = _dot(a, v_bd)
            for i in range(2):
                h = 2 * p + i
                sl = slice(h * GROUP, (h + 1) * GROUP)
                o = inter[h] + o_pair[:, i * GROUP:(i + 1) * GROUP]
                y_ref[rows, sl] = (_rms_scale(o) * onorm_g_ref[:, sl] * ops["gate"][:, sl]).astype(y_ref.dtype)
        return new_state, mid

    def run_tile(stage3):
        state = [state_ref[h] for h in range(heads)]
        g = {c: gates(c) for c in range(min(2, n_chunks))}
        b = {c: cumulative_decay(g[c]) for c in g}
        ops = stage3(g[0], b[0])
        for c in range(n_chunks):
            if c + 2 < n_chunks:
                g[c + 2] = gates(c + 2)
            nxt = stage3(g[c + 1], b[c + 1]) if c + 1 < n_chunks else None
            mid = (lambda c=c: cumulative_decay(g[c + 2])) if c + 2 < n_chunks else (lambda: None)
            state, b[c + 2] = recur(c, ops, state, mid)
            ops = nxt
        for h in range(heads):
            state_ref[h] = state[h]

    gates_bounded = jnp.min(lb) >= MIN_GATE_FOR_SINGLE_PRODUCT

    @pl.when(gates_bounded)
    def _():
        run_tile(scaled_operands_bounded)

    @pl.when(jnp.logical_not(gates_bounded))
    def _():
        run_tile(scaled_operands)


def _output_kernel(x_ref, yc_ref, yh_ref, p_ref, w_out_c_ref, w_out_h_ref, pe_g_ref, w_pg_ref,
                   w_pp_ref, final_g_ref, out_ref):
    h = x_ref[...] + _dot(yc_ref[...], w_out_c_ref[...]) + _dot(yh_ref[...], w_out_h_ref[...])
    pe = _dot(p_ref[...].astype(BF16), w_pp_ref[...])
    gate = _sigmoid(_dot((_rms_scale(h) * pe_g_ref[...]).astype(BF16), w_pg_ref[...]))
    h = h + gate * pe
    out_ref[...] = _rms_scale(h) * final_g_ref[...]


def _full(shape):
    return pl.BlockSpec(shape, lambda b, t: (0,) * len(shape))


def _layer(x, p_i, ln_g, w_in, conv_w, conv_b, cn_g, cn_b, w_pw2, b_pw2, lb_logits, onorm_g,
           w_out, pe_g, w_pg, w_pp, out_g, layer):
    batch, seq, d_model = x.shape
    width = conv_w.shape[-1]
    ple = p_i.shape[-1]
    tile = min(TIME_TILE, seq)
    assert seq % tile == 0 and tile % CHUNK == 0 and width % (2 * GROUP) == 0
    assert tile % (CONV_SLABS * SUBLANES) == 0 and width % CONV_LANES == 0
    grid = (batch, seq // tile)
    row2 = lambda a: a.reshape(1, -1)
    tok = lambda w: pl.BlockSpec((None, tile, w), lambda b, t: (b, t, 0))
    params = pltpu.CompilerParams(dimension_semantics=("arbitrary", "arbitrary"),
                                  vmem_limit_bytes=VMEM_LIMIT_BYTES)

    w_in_c = w_in[:, :3 * width].astype(BF16)
    w_in_h = w_in[:, 3 * width:].astype(BF16)

    y_conv = pl.pallas_call(
        functools.partial(_conv_branch_kernel, tile=tile, width=width),
        grid=grid,
        in_specs=[tok(d_model), _full((1, d_model)), _full((d_model, 3 * width)),
                  _full((CONV_K, width)), _full((1, width)), _full((1, width)), _full((1, width)),
                  _full((width, width)), _full((1, width))],
        out_specs=tok(width),
        out_shape=jax.ShapeDtypeStruct((batch, seq, width), BF16),
        scratch_shapes=[pltpu.VMEM((HALO + tile, width), F32),
                        pltpu.VMEM((SUBLANES * CONV_K, width), F32)],
        compiler_params=params,
        name="conv_branch",
    )(x, row2(ln_g), w_in_c, conv_w, row2(conv_b), row2(cn_g), row2(cn_b),
      w_pw2.astype(BF16), row2(b_pw2))

    tri = jnp.asarray(np.tril(np.ones((CHUNK, CHUNK), np.float32)), BF16)
    y_hgrn = pl.pallas_call(
        functools.partial(_hgrn_branch_kernel, tile=tile, width=width, layer=layer),
        grid=grid,
        in_specs=[tok(d_model), _full((1, d_model)), _full((d_model, 4 * width)),
                  _full(lb_logits.shape), _full((1, width)), _full(tri.shape)],
        out_specs=tok(width),
        out_shape=jax.ShapeDtypeStruct((batch, seq, width), BF16),
        scratch_shapes=[pltpu.VMEM((width // GROUP, GROUP, GROUP), F32)],
        compiler_params=params,
        name="hgrn_branch",
    )(x, row2(ln_g), w_in_h, lb_logits, row2(onorm_g), tri)

    return pl.pallas_call(
        _output_kernel,
        grid=grid,
        in_specs=[tok(d_model), tok(width), tok(width), tok(ple),
                  _full((width, d_model)), _full((width, d_model)), _full((1, d_model)),
                  _full((d_model, d_model)), _full((ple, d_model)), _full((1, d_model))],
        out_specs=tok(d_model),
        out_shape=jax.ShapeDtypeStruct((batch, seq, d_model), x.dtype),
        compiler_params=params,
        name="output_proj",
    )(x, y_conv, y_hgrn, p_i, w_out[:width].astype(BF16), w_out[width:].astype(BF16),
      row2(pe_g), w_pg.astype(BF16), w_pp.astype(BF16), row2(out_g))


def kernel(x, p, ln_g, w_in, conv_w, conv_b, cnorm_g, cnorm_b, w_pw2, b_pw2, lb_logits, onorm_g,
           w_out, pe_norm_g, w_pg, w_pp, final_g):
    depth = p.shape[0]
    assert depth == 1, "the final norm is fused into the only layer's output kernel"
    return _layer(x, p[0], ln_g[0], w_in[0], conv_w[0], conv_b[0], cnorm_g[0], cnorm_b[0],
                  w_pw2[0], b_pw2[0], lb_logits, onorm_g[0], w_out[0], pe_norm_g[0], w_pg[0],
                  w_pp[0], final_g, layer=0)
```

```python
import functools

import numpy as np
import jax
import jax.numpy as jnp
from jax import lax
from jax.experimental import pallas as pl
from jax.experimental.pallas import tpu as pltpu

EPS = 1e-6
CONV_K = 31
GROUP = 128
CHUNK = 64
LOG2_CHUNK = 6
SUBLANES = 8
HALO = 32
CONV_LANES = 256
CONV_UNROLL = 4
MIN_GATE_FOR_SINGLE_PRODUCT = 0.125
TIME_TILE = 512
VMEM_LIMIT_BYTES = 56 * 1024 * 1024

HIGH_HALF = np.uint32(0xFFFF0000)

BF16 = jnp.bfloat16
F32 = jnp.float32


def _rms_scale(x):
    return x * lax.rsqrt(jnp.mean(x * x, axis=-1, keepdims=True) + EPS)


def _sigmoid(x):
    return jax.nn.sigmoid(x)


def _dot(a, b):
    return jnp.dot(a, b, preferred_element_type=F32)


def _dot_nt(a, b):
    return lax.dot_general(a, b, (((1,), (1,)), ((), ())), preferred_element_type=F32)


def _dot_tn(a, b):
    return lax.dot_general(a, b, (((0,), (0,)), ((), ())), preferred_element_type=F32)


def _conv_branch_kernel(x_ref, ln_g_ref, w_in_ref, conv_w_ref, conv_b_ref, cn_g_ref, cn_b_ref,
                        w_pw2_ref, b_pw2_ref, y_ref, vbuf_ref, conv_ref, wb_ref, *, tile, width):
    t_idx = pl.program_id(1)
    half = tile // 2
    packed_rows = 2 * SUBLANES

    u = (_rms_scale(x_ref[...]) * ln_g_ref[...]).astype(BF16)
    z = _dot(u, w_in_ref[...])
    v = z[:, :width] * _sigmoid(z[:, width:2 * width])

    @pl.when(t_idx == 0)
    def _():
        vbuf_ref[0:HALO, :] = jnp.zeros((HALO, width), F32)

    vbuf_ref[HALO:HALO + tile, :] = v

    @pl.when((pl.program_id(0) == 0) & (t_idx == 0))
    def _():
        for j in range(CONV_K):
            wb_ref[packed_rows * j:packed_rows * (j + 1), :] = jnp.broadcast_to(
                conv_w_ref[j:j + 1, :], (packed_rows, width)).astype(BF16)

    row_in_block = lax.broadcasted_iota(jnp.int32, (packed_rows, CONV_LANES), 0) & (SUBLANES - 1)
    for lt in range(width // CONV_LANES):
        lanes = slice(lt * CONV_LANES, (lt + 1) * CONV_LANES)
        bias = jnp.broadcast_to(conv_b_ref[:, lanes], (SUBLANES, CONV_LANES))

        def packed_block(row):
            return jnp.concatenate([vbuf_ref[pl.ds(HALO + row, SUBLANES), lanes],
                                    vbuf_ref[pl.ds(HALO + half + row, SUBLANES), lanes]],
                                   axis=0).astype(BF16)

        def partial_sums(blocks):
            sums = []
            for r in range(SUBLANES):
                g = None
                for a, blk in enumerate(blocks):
                    s = SUBLANES * a + r
                    if s < CONV_K:
                        j = CONV_K - 1 - s
                        term = blk * wb_ref[packed_rows * j:packed_rows * (j + 1), lanes]
                        g = term if g is None else g + term
                sums.append(g)
            return sums

        def conv_rows(i, carry):
            prev, older = carry
            r0 = pl.multiple_of(i * SUBLANES, SUBLANES)
            blocks = (packed_block(r0),) + older
            cur = partial_sums(blocks)
            y = [cur[0][:SUBLANES].astype(F32) + bias, cur[0][SUBLANES:].astype(F32) + bias]
            for r in range(1, SUBLANES):
                g = jnp.where(row_in_block < SUBLANES - r, cur[r], prev[r - 1]).astype(F32)
                y[0] = y[0] + pltpu.roll(g[:SUBLANES], r, axis=0)
                y[1] = y[1] + pltpu.roll(g[SUBLANES:], r, axis=0)
            conv_ref[pl.ds(r0, SUBLANES), lanes] = y[0]
            conv_ref[pl.ds(half + r0, SUBLANES), lanes] = y[1]
            return tuple(cur[1:]), blocks[:-1]

        history = tuple(packed_block(-SUBLANES * a) for a in range(1, HALO // SUBLANES + 1))
        lax.fori_loop(0, half // SUBLANES, conv_rows,
                      (tuple(partial_sums(history)[1:]), history[:-1]), unroll=CONV_UNROLL)

    vbuf_ref[0:HALO, :] = vbuf_ref[tile:tile + HALO, :]

    for g in range(width // GROUP):
        sl = slice(g * GROUP, (g + 1) * GROUP)
        yg = conv_ref[:, sl]
        d = yg - jnp.mean(yg, axis=-1, keepdims=True)
        yn = d * lax.rsqrt(jnp.mean(d * d, axis=-1, keepdims=True) + EPS)
        yn = yn * cn_g_ref[:, sl] + cn_b_ref[:, sl]
        conv_ref[:, sl] = yn * _sigmoid(yn)

    y = _dot(conv_ref[...].astype(BF16), w_pw2_ref[...]) + b_pw2_ref[...]
    z_gate = z[:, 2 * width:]
    y_ref[...] = (y * (z_gate * _sigmoid(z_gate))).astype(y_ref.dtype)


def _boundary_rows(b, j):
    half = 1 << j
    rows, width = b.shape
    if 2 * half >= SUBLANES:
        return jnp.concatenate(
            [jnp.broadcast_to(b[s + half - 1:s + half, :], (2 * half, width))
             for s in range(0, rows, 2 * half)], axis=0)
    assert 4 * half == SUBLANES
    sub = lax.broadcasted_iota(jnp.int32, (SUBLANES, width), 0)
    return jnp.concatenate(
        [jnp.where(sub < 2 * half,
                   jnp.broadcast_to(b[s + half - 1:s + half, :], (SUBLANES, width)),
                   jnp.broadcast_to(b[s + 3 * half - 1:s + 3 * half, :], (SUBLANES, width)))
         for s in range(0, rows, SUBLANES)], axis=0)


def _block_diag_pair(m):
    zero = jnp.zeros((m.shape[0], GROUP), m.dtype)
    return jnp.concatenate([jnp.concatenate([m[:, :GROUP], zero], axis=1),
                            jnp.concatenate([zero, m[:, GROUP:]], axis=1)], axis=0)


def _hgrn_branch_kernel(x_ref, ln_g_ref, w_in_ref, lb_logits_ref, onorm_g_ref, tri_ref, y_ref,
                        state_ref, *, tile, width, layer):
    heads = width // GROUP
    n_chunks = tile // CHUNK

    @pl.when(pl.program_id(1) == 0)
    def _():
        state_ref[...] = jnp.zeros(state_ref.shape, F32)

    u = (_rms_scale(x_ref[...]) * ln_g_ref[...]).astype(BF16)
    z = _dot(u, w_in_ref[...])

    logits = lb_logits_ref[...]
    e = jnp.exp(logits - jnp.max(logits, axis=0, keepdims=True))
    lb = jnp.sum(e[:layer + 1], axis=0, keepdims=True) / jnp.sum(e, axis=0, keepdims=True)

    row = lax.broadcasted_iota(jnp.int32, (CHUNK, 2 * CHUNK), 0)
    col = lax.broadcasted_iota(jnp.int32, (CHUNK, 2 * CHUNK), 1) & (CHUNK - 1)
    row_xor_col = row ^ col
    causal = row > col
    level_masks = [causal & ((row_xor_col >> j) == 1) for j in range(LOG2_CHUNK)]
    diag_mask = row == col
    odd_row = (lax.broadcasted_iota(jnp.int32, (CHUNK, width), 0) & 1) == 1
    tri = tri_ref[...]

    def gates(c):
        rows = slice(c * CHUNK, (c + 1) * CHUNK)
        zq = z[rows, :width]
        sig_f = _sigmoid(z[rows, width:2 * width])
        zg = z[rows, 3 * width:]
        lf = jnp.log(lb + (1.0 - lb) * sig_f)
        lf_top = lax.bitcast_convert_type(lax.bitcast_convert_type(lf, jnp.uint32) & HIGH_HALF, F32)
        return dict(q=zq * _sigmoid(zq),
                    k=(1.0 - lb) * (1.0 - sig_f),
                    lf=lf, lf_top=lf_top.astype(BF16), lf_rest=(lf - lf_top).astype(BF16),
                    v=z[rows, 2 * width:3 * width].astype(BF16), gate=zg * _sigmoid(zg))

    def cumulative_decay(g):
        return _dot(tri, g["lf_top"]) + _dot(tri, g["lf_rest"])

    def scaled_operands_bounded(g, b):
        q, k = g["q"], g["k"]
        b_mid = b[CHUNK // 2 - 1:CHUNK // 2, :]
        b_last = b[CHUNK - 1:CHUNK, :]
        q_mid = q * jnp.exp(b - b_mid)
        k_mid = k * jnp.exp(b_mid - b)
        return dict(q_lvl=[q_mid.astype(BF16)], k_lvl=[k_mid.astype(BF16)],
                    q_dec=(q_mid * jnp.exp(b_mid)).astype(BF16),
                    k_dec=(k_mid * jnp.exp(b_last - b_mid)).astype(BF16), e_last=jnp.exp(b_last),
                    v=g["v"], gate=g["gate"])

    def scaled_operands(g, b):
        q, k, lf = g["q"], g["k"], g["lf"]
        e_b = jnp.exp(b)
        b_last = b[CHUNK - 1:CHUNK, :]
        factors = [jnp.where(odd_row, jnp.exp(lf), 1.0)]
        for j in range(1, LOG2_CHUNK):
            factors.append(jnp.exp(-jnp.abs(b - _boundary_rows(b, j))))
        q16, k16 = q.astype(BF16), k.astype(BF16)
        q_lvl, k_lvl = [q16], [k16]
        for w in factors:
            w16 = w.astype(BF16)
            q_lvl.append(q16 * w16)
            k_lvl.append(k16 * w16)
        return dict(q_lvl=q_lvl, k_lvl=k_lvl, q_dec=(q * e_b).astype(BF16),
                    k_dec=(k * jnp.exp(b_last - b)).astype(BF16), e_last=jnp.exp(b_last),
                    v=g["v"], gate=g["gate"])

    def recur(c, ops, state, mid_mxu_work):
        rows = slice(c * CHUNK, (c + 1) * CHUNK)
        intra = []
        for p in range(heads // 2):
            pair = slice(2 * p * GROUP, 2 * (p + 1) * GROUP)
            prods = [_dot_nt(ql[:, pair], _block_diag_pair(kl[:, pair]))
                     for ql, kl in zip(ops["q_lvl"], ops["k_lvl"])]
            if len(prods) == 1:
                a = jnp.where(causal | diag_mask, prods[0], 0.0)
            else:
                a = jnp.where(diag_mask, prods[0], 0.0)
                for j in range(LOG2_CHUNK):
                    a = jnp.where(level_masks[j], prods[1 + j], a)
            intra.append((a.astype(BF16), _block_diag_pair(ops["v"][:, pair])))
        inter, new_state = [], []
        for h in range(heads):
            sl = slice(h * GROUP, (h + 1) * GROUP)
            s_t = state[h]
            inter.append(_dot_nt(ops["q_dec"][:, sl], s_t.astype(BF16)))
            new_state.append(ops["e_last"][:, sl] * s_t + _dot_tn(ops["v"][:, sl], ops["k_dec"][:, sl]))
        mid = mid_mxu_work()
        for p, (a, v_bd) in enumerate(intra):
            o_pair = _dot(a, v_bd)
            for i in range(2):
                h = 2 * p + i
                sl = slice(h * GROUP, (h + 1) * GROUP)
                o = inter[h] + o_pair[:, i * GROUP:(i + 1) * GROUP]
                y_ref[rows, sl] = (_rms_scale(o) * onorm_g_ref[:, sl] * ops["gate"][:, sl]).astype(y_ref.dtype)
        return new_state, mid

    def run_tile(stage3):
        state = [state_ref[h] for h in range(heads)]
        g = {c: gates(c) for c in range(min(2, n_chunks))}
        b = {c: cumulative_decay(g[c]) for c in g}
        ops = stage3(g[0], b[0])
        for c in range(n_chunks):
            if c + 2 < n_chunks:
                g[c + 2] = gates(c + 2)
            nxt = stage3(g[c + 1], b[c + 1]) if c + 1 < n_chunks else None
            mid = (lambda c=c: cumulative_decay(g[c + 2])) if c + 2 < n_chunks else (lambda: None)
            state, b[c + 2] = recur(c, ops, state, mid)
            ops = nxt
        for h in range(heads):
            state_ref[h] = state[h]

    gates_bounded = jnp.min(lb) >= MIN_GATE_FOR_SINGLE_PRODUCT

    @pl.when(gates_bounded)
    def _():
        run_tile(scaled_operands_bounded)

    @pl.when(jnp.logical_not(gates_bounded))
    def _():
        run_tile(scaled_operands)


def _output_kernel(x_ref, yc_ref, yh_ref, p_ref, w_out_c_ref, w_out_h_ref, pe_g_ref, w_pg_ref,
                   w_pp_ref, final_g_ref, out_ref):
    h = x_ref[...] + _dot(yc_ref[...], w_out_c_ref[...]) + _dot(yh_ref[...], w_out_h_ref[...])
    pe = _dot(p_ref[...].astype(BF16), w_pp_ref[...])
    gate = _sigmoid(_dot((_rms_scale(h) * pe_g_ref[...]).astype(BF16), w_pg_ref[...]))
    h = h + gate * pe
    out_ref[...] = _rms_scale(h) * final_g_ref[...]


def _full(shape):
    return pl.BlockSpec(shape, lambda b, t: (0,) * len(shape))


def _layer(x, p_i, ln_g, w_in, conv_w, conv_b, cn_g, cn_b, w_pw2, b_pw2, lb_logits, onorm_g,
           w_out, pe_g, w_pg, w_pp, out_g, layer):
    batch, seq, d_model = x.shape
    width = conv_w.shape[-1]
    ple = p_i.shape[-1]
    tile = min(TIME_TILE, seq)
    assert seq % tile == 0 and tile % CHUNK == 0 and width % (2 * GROUP) == 0
    assert tile % (2 * SUBLANES * CONV_UNROLL) == 0 and tile // 2 >= HALO and width % CONV_LANES == 0
    grid = (batch, seq // tile)
    row2 = lambda a: a.reshape(1, -1)
    tok = lambda w: pl.BlockSpec((None, tile, w), lambda b, t: (b, t, 0))
    params = pltpu.CompilerParams(dimension_semantics=("arbitrary", "arbitrary"),
                                  vmem_limit_bytes=VMEM_LIMIT_BYTES)

    w_in_c = w_in[:, :3 * width].astype(BF16)
    w_in_h = w_in[:, 3 * width:].astype(BF16)

    y_conv = pl.pallas_call(
        functools.partial(_conv_branch_kernel, tile=tile, width=width),
        grid=grid,
        in_specs=[tok(d_model), _full((1, d_model)), _full((d_model, 3 * width)),
                  _full((CONV_K, width)), _full((1, width)), _full((1, width)), _full((1, width)),
                  _full((width, width)), _full((1, width))],
        out_specs=tok(width),
        out_shape=jax.ShapeDtypeStruct((batch, seq, width), BF16),
        scratch_shapes=[pltpu.VMEM((HALO + tile, width), F32), pltpu.VMEM((tile, width), F32),
                        pltpu.VMEM((2 * SUBLANES * CONV_K, width), BF16)],
        compiler_params=params,
        name="conv_branch",
    )(x, row2(ln_g), w_in_c, conv_w, row2(conv_b), row2(cn_g), row2(cn_b),
      w_pw2.astype(BF16), row2(b_pw2))

    tri = jnp.asarray(np.tril(np.ones((CHUNK, CHUNK), np.float32)), BF16)
    y_hgrn = pl.pallas_call(
        functools.partial(_hgrn_branch_kernel, tile=tile, width=width, layer=layer),
        grid=grid,
        in_specs=[tok(d_model), _full((1, d_model)), _full((d_model, 4 * width)),
                  _full(lb_logits.shape), _full((1, width)), _full(tri.shape)],
        out_specs=tok(width),
        out_shape=jax.ShapeDtypeStruct((batch, seq, width), BF16),
        scratch_shapes=[pltpu.VMEM((width // GROUP, GROUP, GROUP), F32)],
        compiler_params=params,
        name="hgrn_branch",
    )(x, row2(ln_g), w_in_h, lb_logits, row2(onorm_g), tri)

    return pl.pallas_call(
        _output_kernel,
        grid=grid,
        in_specs=[tok(d_model), tok(width), tok(width), tok(ple),
                  _full((width, d_model)), _full((width, d_model)), _full((1, d_model)),
                  _full((d_model, d_model)), _full((ple, d_model)), _full((1, d_model))],
        out_specs=tok(d_model),
        out_shape=jax.ShapeDtypeStruct((batch, seq, d_model), x.dtype),
        compiler_params=params,
        name="output_proj",
    )(x, y_conv, y_hgrn, p_i, w_out[:width].astype(BF16), w_out[width:].astype(BF16),
      row2(pe_g), w_pg.astype(BF16), w_pp.astype(BF16), row2(out_g))


def kernel(x, p, ln_g, w_in, conv_w, conv_b, cnorm_g, cnorm_b, w_pw2, b_pw2, lb_logits, onorm_g,
           w_out, pe_norm_g, w_pg, w_pp, final_g):
    depth = p.shape[0]
    assert depth == 1, "the final norm is fused into the only layer's output kernel"
    return _layer(x, p[0], ln_g[0], w_in[0], conv_w[0], conv_b[0], cnorm_g[0], cnorm_b[0],
                  w_pw2[0], b_pw2[0], lb_logits, onorm_g[0], w_out[0], pe_norm_g[0], w_pg[0],
                  w_pp[0], final_g, layer=0)
```

```python
import functools

import numpy as np
import jax
import jax.numpy as jnp
from jax import lax
from jax.experimental import pallas as pl
from jax.experimental.pallas import tpu as pltpu

EPS = 1e-6
CONV_K = 31
GROUP = 128
CHUNK = 64
LOG2_CHUNK = 6
SUBLANES = 8
HALO = 32
CONV_LANES = 256
CONV_UNROLL = 4
MIN_GATE_FOR_SINGLE_PRODUCT = 0.125
TIME_TILE = 512
VMEM_LIMIT_BYTES = 56 * 1024 * 1024

HIGH_HALF = np.uint32(0xFFFF0000)

BF16 = jnp.bfloat16
F32 = jnp.float32


def _rms_scale(x):
    return x * lax.rsqrt(jnp.mean(x * x, axis=-1, keepdims=True) + EPS)


def _sigmoid(x):
    return jax.nn.sigmoid(x)


def _dot(a, b):
    return jnp.dot(a, b, preferred_element_type=F32)


def _dot_nt(a, b):
    return lax.dot_general(a, b, (((1,), (1,)), ((), ())), preferred_element_type=F32)


def _dot_tn(a, b):
    return lax.dot_general(a, b, (((0,), (0,)), ((), ())), preferred_element_type=F32)


def _conv_branch_kernel(x_ref, ln_g_ref, w_in_ref, conv_w_ref, conv_b_ref, cn_g_ref, cn_b_ref,
                        w_pw2_ref, b_pw2_ref, y_ref, vbuf_ref, conv_ref, wb_ref, *, tile, width):
    t_idx = pl.program_id(1)
    halves = (slice(0, tile // 2), slice(tile // 2, tile))

    @pl.when(t_idx == 0)
    def _():
        vbuf_ref[0:HALO, :] = jnp.zeros((HALO, width), F32)

    @pl.when((pl.program_id(0) == 0) & (t_idx == 0))
    def _():
        for j in range(CONV_K):
            wb_ref[SUBLANES * j:SUBLANES * (j + 1), :] = jnp.broadcast_to(
                conv_w_ref[j:j + 1, :], (SUBLANES, width))

    z_gate = []
    for rows in halves:
        u = (_rms_scale(x_ref[rows, :]) * ln_g_ref[...]).astype(BF16)
        z = _dot(u, w_in_ref[...])
        vbuf_ref[HALO + rows.start:HALO + rows.stop, :] = z[:, :width] * _sigmoid(z[:, width:2 * width])
        z_gate.append(z[:, 2 * width:])

    sub = lax.broadcasted_iota(jnp.int32, (SUBLANES, CONV_LANES), 0)
    for lt in range(width // CONV_LANES):
        lanes = slice(lt * CONV_LANES, (lt + 1) * CONV_LANES)
        bias = jnp.broadcast_to(conv_b_ref[:, lanes], (SUBLANES, CONV_LANES))

        def partial_sums(row):
            blocks = [vbuf_ref[pl.ds(HALO + row - SUBLANES * a, SUBLANES), lanes]
                      for a in range(HALO // SUBLANES)]
            sums = []
            for r in range(SUBLANES):
                g = None
                for a, blk in enumerate(blocks):
                    s = SUBLANES * a + r
                    if s < CONV_K:
                        j = CONV_K - 1 - s
                        term = blk * wb_ref[SUBLANES * j:SUBLANES * (j + 1), lanes]
                        g = term if g is None else g + term
                sums.append(g)
            return sums

        def conv_rows(i, prev):
            r0 = pl.multiple_of(i * SUBLANES, SUBLANES)
            cur = partial_sums(r0)
            y = cur[0] + bias
            for r in range(1, SUBLANES):
                y = y + pltpu.roll(jnp.where(sub < SUBLANES - r, cur[r], prev[r - 1]), r, axis=0)
            conv_ref[pl.ds(r0, SUBLANES), lanes] = y
            return tuple(cur[1:])

        lax.fori_loop(0, tile // SUBLANES, conv_rows, tuple(partial_sums(-SUBLANES)[1:]),
                      unroll=CONV_UNROLL)

    vbuf_ref[0:HALO, :] = vbuf_ref[tile:tile + HALO, :]

    for rows, zg in zip(halves, z_gate):
        normed = []
        for g in range(width // GROUP):
            sl = slice(g * GROUP, (g + 1) * GROUP)
            yg = conv_ref[rows, sl]
            d = yg - jnp.mean(yg, axis=-1, keepdims=True)
            yn = d * lax.rsqrt(jnp.mean(d * d, axis=-1, keepdims=True) + EPS)
            yn = yn * cn_g_ref[:, sl] + cn_b_ref[:, sl]
            normed.append((yn * _sigmoid(yn)).astype(BF16))
        y = _dot(jnp.concatenate(normed, axis=1), w_pw2_ref[...]) + b_pw2_ref[...]
        y_ref[rows, :] = (y * (zg * _sigmoid(zg))).astype(y_ref.dtype)


def _boundary_rows(b, j):
    half = 1 << j
    rows, width = b.shape
    if 2 * half >= SUBLANES:
        return jnp.concatenate(
            [jnp.broadcast_to(b[s + half - 1:s + half, :], (2 * half, width))
             for s in range(0, rows, 2 * half)], axis=0)
    assert 4 * half == SUBLANES
    sub = lax.broadcasted_iota(jnp.int32, (SUBLANES, width), 0)
    return jnp.concatenate(
        [jnp.where(sub < 2 * half,
                   jnp.broadcast_to(b[s + half - 1:s + half, :], (SUBLANES, width)),
                   jnp.broadcast_to(b[s + 3 * half - 1:s + 3 * half, :], (SUBLANES, width)))
         for s in range(0, rows, SUBLANES)], axis=0)


def _block_diag_pair(m):
    zero = jnp.zeros((m.shape[0], GROUP), m.dtype)
    return jnp.concatenate([jnp.concatenate([m[:, :GROUP], zero], axis=1),
                            jnp.concatenate([zero, m[:, GROUP:]], axis=1)], axis=0)


def _hgrn_branch_kernel(x_ref, ln_g_ref, w_in_ref, lb_logits_ref, onorm_g_ref, tri_ref, y_ref,
                        state_ref, *, tile, width, layer):
    heads = width // GROUP
    n_chunks = tile // CHUNK

    @pl.when(pl.program_id(1) == 0)
    def _():
        state_ref[...] = jnp.zeros(state_ref.shape, F32)

    u = (_rms_scale(x_ref[...]) * ln_g_ref[...]).astype(BF16)
    z = _dot(u, w_in_ref[...])

    logits = lb_logits_ref[...]
    e = jnp.exp(logits - jnp.max(logits, axis=0, keepdims=True))
    lb = jnp.sum(e[:layer + 1], axis=0, keepdims=True) / jnp.sum(e, axis=0, keepdims=True)

    row = lax.broadcasted_iota(jnp.int32, (CHUNK, 2 * CHUNK), 0)
    col = lax.broadcasted_iota(jnp.int32, (CHUNK, 2 * CHUNK), 1) & (CHUNK - 1)
    row_xor_col = row ^ col
    causal = row > col
    level_masks = [causal & ((row_xor_col >> j) == 1) for j in range(LOG2_CHUNK)]
    diag_mask = row == col
    odd_row = (lax.broadcasted_iota(jnp.int32, (CHUNK, width), 0) & 1) == 1
    tri_twice = tri_ref[...]

    def gates(c):
        rows = slice(c * CHUNK, (c + 1) * CHUNK)
        zq = z[rows, :width]
        sig_f = _sigmoid(z[rows, width:2 * width])
        zg = z[rows, 3 * width:]
        lf = jnp.log(lb + (1.0 - lb) * sig_f)
        lf_top = lax.bitcast_convert_type(lax.bitcast_convert_type(lf, jnp.uint32) & HIGH_HALF, F32)
        lf_pieces = jnp.concatenate([lf_top.astype(BF16), (lf - lf_top).astype(BF16)], axis=0)
        return dict(q=zq * _sigmoid(zq),
                    k=(1.0 - lb) * (1.0 - sig_f),
                    lf=lf, lf_pieces=lf_pieces,
                    v=z[rows, 2 * width:3 * width].astype(BF16), gate=zg * _sigmoid(zg))

    def cumulative_decay(g):
        return _dot(tri_twice, g["lf_pieces"])

    def scaled_operands_bounded(g, b):
        q, k = g["q"], g["k"]
        b_mid = b[CHUNK // 2 - 1:CHUNK // 2, :]
        b_last = b[CHUNK - 1:CHUNK, :]
        q_mid = q * jnp.exp(b - b_mid)
        k_mid = k * jnp.exp(b_mid - b)
        return dict(q_lvl=[q_mid.astype(BF16)], k_lvl=[k_mid.astype(BF16)],
                    q_dec=(q_mid * jnp.exp(b_mid)).astype(BF16),
                    k_dec=(k_mid * jnp.exp(b_last - b_mid)).astype(BF16), e_last=jnp.exp(b_last),
                    v=g["v"], gate=g["gate"])

    def scaled_operands(g, b):
        q, k, lf = g["q"], g["k"], g["lf"]
        e_b = jnp.exp(b)
        b_last = b[CHUNK - 1:CHUNK, :]
        factors = [jnp.where(odd_row, jnp.exp(lf), 1.0)]
        for j in range(1, LOG2_CHUNK):
            factors.append(jnp.exp(-jnp.abs(b - _boundary_rows(b, j))))
        q16, k16 = q.astype(BF16), k.astype(BF16)
        q_lvl, k_lvl = [q16], [k16]
        for w in factors:
            w16 = w.astype(BF16)
            q_lvl.append(q16 * w16)
            k_lvl.append(k16 * w16)
        return dict(q_lvl=q_lvl, k_lvl=k_lvl, q_dec=(q * e_b).astype(BF16),
                    k_dec=(k * jnp.exp(b_last - b)).astype(BF16), e_last=jnp.exp(b_last),
                    v=g["v"], gate=g["gate"])

    def recur(c, ops, state, mid_mxu_work):
        rows = slice(c * CHUNK, (c + 1) * CHUNK)
        intra = []
        for p in range(heads // 2):
            pair = slice(2 * p * GROUP, 2 * (p + 1) * GROUP)
            prods = [_dot_nt(ql[:, pair], _block_diag_pair(kl[:, pair]))
                     for ql, kl in zip(ops["q_lvl"], ops["k_lvl"])]
            if len(prods) == 1:
                a = jnp.where(causal | diag_mask, prods[0], 0.0)
            else:
                a = jnp.where(diag_mask, prods[0], 0.0)
                for j in range(LOG2_CHUNK):
                    a = jnp.where(level_masks[j], prods[1 + j], a)
            intra.append((a.astype(BF16), _block_diag_pair(ops["v"][:, pair])))
        inter, new_state = [], []
        for h in range(heads):
            sl = slice(h * GROUP, (h + 1) * GROUP)
            s_t = state[h]
            inter.append(_dot_nt(ops["q_dec"][:, sl], s_t.astype(BF16)))
            new_state.append(ops["e_last"][:, sl] * s_t + _dot_tn(ops["v"][:, sl], ops["k_dec"][:, sl]))
        mid = mid_mxu_work()
        for p, (a, v_bd) in enumerate(intra):
            o_pair = _dot(a, v_bd)
            for i in range(2):
                h = 2 * p + i
                sl = slice(h * GROUP, (h + 1) * GROUP)
                o = inter[h] + o_pair[:, i * GROUP:(i + 1) * GROUP]
                y_ref[rows, sl] = (_rms_scale(o) * onorm_g_ref[:, sl] * ops["gate"][:, sl]).astype(y_ref.dtype)
        return new_state, mid

    def run_tile(stage3):
        state = [state_ref[h] for h in range(heads)]
        g = {c: gates(c) for c in range(min(2, n_chunks))}
        b = {c: cumulative_decay(g[c]) for c in g}
        ops = stage3(g[0], b[0])
        for c in range(n_chunks):
            if c + 2 < n_chunks:
                g[c + 2] = gates(c + 2)
            nxt = stage3(g[c + 1], b[c + 1]) if c + 1 < n_chunks else None
            mid = (lambda c=c: cumulative_decay(g[c + 2])) if c + 2 < n_chunks else (lambda: None)
            state, b[c + 2] = recur(c, ops, state, mid)
            ops = nxt
        for h in range(heads):
            state_ref[h] = state[h]

    gates_bounded = jnp.min(lb) >= MIN_GATE_FOR_SINGLE_PRODUCT

    @pl.when(gates_bounded)
    def _():
        run_tile(scaled_operands_bounded)

    @pl.when(jnp.logical_not(gates_bounded))
    def _():
        run_tile(scaled_operands)


def _output_kernel(x_ref, yc_ref, yh_ref, p_ref, w_out_c_ref, w_out_h_ref, pe_g_ref, w_pg_ref,
                   w_pp_ref, final_g_ref, out_ref):
    h = x_ref[...] + _dot(yc_ref[...], w_out_c_ref[...]) + _dot(yh_ref[...], w_out_h_ref[...])
    pe = _dot(p_ref[...].astype(BF16), w_pp_ref[...])
    gate = _sigmoid(_dot((_rms_scale(h) * pe_g_ref[...]).astype(BF16), w_pg_ref[...]))
    h = h + gate * pe
    out_ref[...] = _rms_scale(h) * final_g_ref[...]


def _full(shape):
    return pl.BlockSpec(shape, lambda b, t: (0,) * len(shape))


def _layer(x, p_i, ln_g, w_in, conv_w, conv_b, cn_g, cn_b, w_pw2, b_pw2, lb_logits, onorm_g,
           w_out, pe_g, w_pg, w_pp, out_g, layer):
    batch, seq, d_model = x.shape
    width = conv_w.shape[-1]
    ple = p_i.shape[-1]
    tile = min(TIME_TILE, seq)
    assert seq % tile == 0 and tile % CHUNK == 0 and width % (2 * GROUP) == 0
    assert tile % (2 * SUBLANES * CONV_UNROLL) == 0 and width % CONV_LANES == 0
    grid = (batch, seq // tile)
    row2 = lambda a: a.reshape(1, -1)
    tok = lambda w: pl.BlockSpec((None, tile, w), lambda b, t: (b, t, 0))
    params = pltpu.CompilerParams(dimension_semantics=("arbitrary", "arbitrary"),
                                  vmem_limit_bytes=VMEM_LIMIT_BYTES)

    w_in_c = w_in[:, :3 * width].astype(BF16)
    w_in_h = w_in[:, 3 * width:].astype(BF16)

    y_conv = pl.pallas_call(
        functools.partial(_conv_branch_kernel, tile=tile, width=width),
        grid=grid,
        in_specs=[tok(d_model), _full((1, d_model)), _full((d_model, 3 * width)),
                  _full((CONV_K, width)), _full((1, width)), _full((1, width)), _full((1, width)),
                  _full((width, width)), _full((1, width))],
        out_specs=tok(width),
        out_shape=jax.ShapeDtypeStruct((batch, seq, width), BF16),
        scratch_shapes=[pltpu.VMEM((HALO + tile, width), F32), pltpu.VMEM((tile, width), F32),
                        pltpu.VMEM((SUBLANES * CONV_K, width), F32)],
        compiler_params=params,
        name="conv_branch",
    )(x, row2(ln_g), w_in_c, conv_w, row2(conv_b), row2(cn_g), row2(cn_b),
      w_pw2.astype(BF16), row2(b_pw2))

    tril = np.tril(np.ones((CHUNK, CHUNK), np.float32))
    tri = jnp.asarray(np.concatenate([tril, tril], axis=1), BF16)
    y_hgrn = pl.pallas_call(
        functools.partial(_hgrn_branch_kernel, tile=tile, width=width, layer=layer),
        grid=grid,
        in_specs=[tok(d_model), _full((1, d_model)), _full((d_model, 4 * width)),
                  _full(lb_logits.shape), _full((1, width)), _full(tri.shape)],
        out_specs=tok(width),
        out_shape=jax.ShapeDtypeStruct((batch, seq, width), BF16),
        scratch_shapes=[pltpu.VMEM((width // GROUP, GROUP, GROUP), F32)],
        compiler_params=params,
        name="hgrn_branch",
    )(x, row2(ln_g), w_in_h, lb_logits, row2(onorm_g), tri)

    return pl.pallas_call(
        _output_kernel,
        grid=grid,
        in_specs=[tok(d_model), tok(width), tok(width), tok(ple),
                  _full((width, d_model)), _full((width, d_model)), _full((1, d_model)),
                  _full((d_model, d_model)), _full((ple, d_model)), _full((1, d_model))],
        out_specs=tok(d_model),
        out_shape=jax.ShapeDtypeStruct((batch, seq, d_model), x.dtype),
        compiler_params=params,
        name="output_proj",
    )(x, y_conv, y_hgrn, p_i, w_out[:width].astype(BF16), w_out[width:].astype(BF16),
      row2(pe_g), w_pg.astype(BF16), w_pp.astype(BF16), row2(out_g))


def kernel(x, p, ln_g, w_in, conv_w, conv_b, cnorm_g, cnorm_b, w_pw2, b_pw2, lb_logits, onorm_g,
           w_out, pe_norm_g, w_pg, w_pp, final_g):
    depth = p.shape[0]
    assert depth == 1, "the final norm is fused into the only layer's output kernel"
    return _layer(x, p[0], ln_g[0], w_in[0], conv_w[0], conv_b[0], cnorm_g[0], cnorm_b[0],
                  w_pw2[0], b_pw2[0], lb_logits, onorm_g[0], w_out[0], pe_norm_g[0], w_pg[0],
                  w_pp[0], final_g, layer=0)
```

```python
import functools

import numpy as np
import jax
import jax.numpy as jnp
from jax import lax
from jax.experimental import pallas as pl
from jax.experimental.pallas import tpu as pltpu

EPS = 1e-6
CONV_K = 31
GROUP = 128
CHUNK = 64
LOG2_CHUNK = 6
SUBLANES = 8
HALO = 32
CONV_LANES = 256
CONV_UNROLL = 4
MIN_GATE_FOR_SINGLE_PRODUCT = 0.125
OUTPUT_ROW_PARTS = 2
OUTPUT_TILE = 1024
TIME_TILE = 512
VMEM_LIMIT_BYTES = 56 * 1024 * 1024

HIGH_HALF = np.uint32(0xFFFF0000)

BF16 = jnp.bfloat16
F32 = jnp.float32


def _rms_scale(x):
    return x * lax.rsqrt(jnp.mean(x * x, axis=-1, keepdims=True) + EPS)


def _sigmoid(x):
    return jax.nn.sigmoid(x)


def _dot(a, b):
    return jnp.dot(a, b, preferred_element_type=F32)


def _dot_nt(a, b):
    return lax.dot_general(a, b, (((1,), (1,)), ((), ())), preferred_element_type=F32)


def _dot_tn(a, b):
    return lax.dot_general(a, b, (((0,), (0,)), ((), ())), preferred_element_type=F32)


def _conv_branch_kernel(x_ref, ln_g_ref, w_in_ref, conv_w_ref, conv_b_ref, cn_g_ref, cn_b_ref,
                        w_pw2_ref, b_pw2_ref, y_ref, vbuf_ref, conv_ref, wb_ref, *, tile, width):
    t_idx = pl.program_id(1)
    halves = (slice(0, tile // 2), slice(tile // 2, tile))

    @pl.when(t_idx == 0)
    def _():
        vbuf_ref[0:HALO, :] = jnp.zeros((HALO, width), F32)

    @pl.when((pl.program_id(0) == 0) & (t_idx == 0))
    def _():
        for j in range(CONV_K):
            wb_ref[SUBLANES * j:SUBLANES * (j + 1), :] = jnp.broadcast_to(
                conv_w_ref[j:j + 1, :], (SUBLANES, width))

    z_gate = []
    for rows in halves:
        u = (_rms_scale(x_ref[rows, :]) * ln_g_ref[...]).astype(BF16)
        z = _dot(u, w_in_ref[...])
        vbuf_ref[HALO + rows.start:HALO + rows.stop, :] = z[:, :width] * _sigmoid(z[:, width:2 * width])
        z_gate.append(z[:, 2 * width:])

    sub = lax.broadcasted_iota(jnp.int32, (SUBLANES, CONV_LANES), 0)
    for lt in range(width // CONV_LANES):
        lanes = slice(lt * CONV_LANES, (lt + 1) * CONV_LANES)
        bias = jnp.broadcast_to(conv_b_ref[:, lanes], (SUBLANES, CONV_LANES))

        def partial_sums(row):
            blocks = [vbuf_ref[pl.ds(HALO + row - SUBLANES * a, SUBLANES), lanes]
                      for a in range(HALO // SUBLANES)]
            sums = []
            for r in range(SUBLANES):
                g = None
                for a, blk in enumerate(blocks):
                    s = SUBLANES * a + r
                    if s < CONV_K:
                        j = CONV_K - 1 - s
                        term = blk * wb_ref[SUBLANES * j:SUBLANES * (j + 1), lanes]
                        g = term if g is None else g + term
                sums.append(g)
            return sums

        def conv_rows(i, prev):
            r0 = pl.multiple_of(i * SUBLANES, SUBLANES)
            cur = partial_sums(r0)
            y = cur[0] + bias
            for r in range(1, SUBLANES):
                y = y + pltpu.roll(jnp.where(sub < SUBLANES - r, cur[r], prev[r - 1]), r, axis=0)
            conv_ref[pl.ds(r0, SUBLANES), lanes] = y
            return tuple(cur[1:])

        lax.fori_loop(0, tile // SUBLANES, conv_rows, tuple(partial_sums(-SUBLANES)[1:]),
                      unroll=CONV_UNROLL)

    vbuf_ref[0:HALO, :] = vbuf_ref[tile:tile + HALO, :]

    for rows, zg in zip(halves, z_gate):
        normed = []
        for g in range(width // GROUP):
            sl = slice(g * GROUP, (g + 1) * GROUP)
            yg = conv_ref[rows, sl]
            d = yg - jnp.mean(yg, axis=-1, keepdims=True)
            yn = d * lax.rsqrt(jnp.mean(d * d, axis=-1, keepdims=True) + EPS)
            yn = yn * cn_g_ref[:, sl] + cn_b_ref[:, sl]
            normed.append((yn * _sigmoid(yn)).astype(BF16))
        y = _dot(jnp.concatenate(normed, axis=1), w_pw2_ref[...]) + b_pw2_ref[...]
        y_ref[rows, :] = (y * (zg * _sigmoid(zg))).astype(y_ref.dtype)


def _boundary_rows(b, j):
    half = 1 << j
    rows, width = b.shape
    if 2 * half >= SUBLANES:
        return jnp.concatenate(
            [jnp.broadcast_to(b[s + half - 1:s + half, :], (2 * half, width))
             for s in range(0, rows, 2 * half)], axis=0)
    assert 4 * half == SUBLANES
    sub = lax.broadcasted_iota(jnp.int32, (SUBLANES, width), 0)
    return jnp.concatenate(
        [jnp.where(sub < 2 * half,
                   jnp.broadcast_to(b[s + half - 1:s + half, :], (SUBLANES, width)),
                   jnp.broadcast_to(b[s + 3 * half - 1:s + 3 * half, :], (SUBLANES, width)))
         for s in range(0, rows, SUBLANES)], axis=0)


def _block_diag_pair(m):
    zero = jnp.zeros((m.shape[0], GROUP), m.dtype)
    return jnp.concatenate([jnp.concatenate([m[:, :GROUP], zero], axis=1),
                            jnp.concatenate([zero, m[:, GROUP:]], axis=1)], axis=0)


def _hgrn_branch_kernel(x_ref, ln_g_ref, w_in_ref, lb_logits_ref, onorm_g_ref, tri_ref, y_ref,
                        state_ref, *, tile, width, layer):
    heads = width // GROUP
    n_chunks = tile // CHUNK

    @pl.when(pl.program_id(1) == 0)
    def _():
        state_ref[...] = jnp.zeros(state_ref.shape, F32)

    z = jnp.concatenate(
        [_dot((_rms_scale(x_ref[rows, :]) * ln_g_ref[...]).astype(BF16), w_in_ref[...])
         for rows in (slice(0, tile // 2), slice(tile // 2, tile))], axis=0)

    logits = lb_logits_ref[...]
    e = jnp.exp(logits - jnp.max(logits, axis=0, keepdims=True))
    lb = jnp.sum(e[:layer + 1], axis=0, keepdims=True) / jnp.sum(e, axis=0, keepdims=True)

    row = lax.broadcasted_iota(jnp.int32, (CHUNK, 2 * CHUNK), 0)
    col = lax.broadcasted_iota(jnp.int32, (CHUNK, 2 * CHUNK), 1) & (CHUNK - 1)
    row_xor_col = row ^ col
    causal = row > col
    level_masks = [causal & ((row_xor_col >> j) == 1) for j in range(LOG2_CHUNK)]
    diag_mask = row == col
    odd_row = (lax.broadcasted_iota(jnp.int32, (CHUNK, width), 0) & 1) == 1
    tri_twice = tri_ref[...]

    def gates(c):
        rows = slice(c * CHUNK, (c + 1) * CHUNK)
        zq = z[rows, :width]
        sig_f = _sigmoid(z[rows, width:2 * width])
        zg = z[rows, 3 * width:]
        lf = jnp.log(lb + (1.0 - lb) * sig_f)
        lf_top = lax.bitcast_convert_type(lax.bitcast_convert_type(lf, jnp.uint32) & HIGH_HALF, F32)
        lf_pieces = jnp.concatenate([lf_top.astype(BF16), (lf - lf_top).astype(BF16)], axis=0)
        return dict(q=zq * _sigmoid(zq),
                    k=(1.0 - lb) * (1.0 - sig_f),
                    lf=lf, lf_pieces=lf_pieces,
                    v=z[rows, 2 * width:3 * width].astype(BF16), gate=zg * _sigmoid(zg))

    def cumulative_decay(g):
        return _dot(tri_twice, g["lf_pieces"])

    def scaled_operands_bounded(g, b):
        q, k = g["q"], g["k"]
        b_mid = b[CHUNK // 2 - 1:CHUNK // 2, :]
        b_last = b[CHUNK - 1:CHUNK, :]
        q_mid = q * jnp.exp(b - b_mid)
        k_mid = k * jnp.exp(b_mid - b)
        return dict(q_lvl=[q_mid.astype(BF16)], k_lvl=[k_mid.astype(BF16)],
                    q_dec=(q_mid * jnp.exp(b_mid)).astype(BF16),
                    k_dec=(k_mid * jnp.exp(b_last - b_mid)).astype(BF16), e_last=jnp.exp(b_last),
                    v=g["v"], gate=g["gate"])

    def scaled_operands(g, b):
        q, k, lf = g["q"], g["k"], g["lf"]
        e_b = jnp.exp(b)
        b_last = b[CHUNK - 1:CHUNK, :]
        factors = [jnp.where(odd_row, jnp.exp(lf), 1.0)]
        for j in range(1, LOG2_CHUNK):
            factors.append(jnp.exp(-jnp.abs(b - _boundary_rows(b, j))))
        q16, k16 = q.astype(BF16), k.astype(BF16)
        q_lvl, k_lvl = [q16], [k16]
        for w in factors:
            w16 = w.astype(BF16)
            q_lvl.append(q16 * w16)
            k_lvl.append(k16 * w16)
        return dict(q_lvl=q_lvl, k_lvl=k_lvl, q_dec=(q * e_b).astype(BF16),
                    k_dec=(k * jnp.exp(b_last - b)).astype(BF16), e_last=jnp.exp(b_last),
                    v=g["v"], gate=g["gate"])

    def recur(c, ops, state, mid_mxu_work):
        rows = slice(c * CHUNK, (c + 1) * CHUNK)
        intra = []
        for p in range(heads // 2):
            pair = slice(2 * p * GROUP, 2 * (p + 1) * GROUP)
            prods = [_dot_nt(ql[:, pair], _block_diag_pair(kl[:, pair]))
                     for ql, kl in zip(ops["q_lvl"], ops["k_lvl"])]
            if len(prods) == 1:
                a = jnp.where(causal | diag_mask, prods[0], 0.0)
            else:
                a = jnp.where(diag_mask, prods[0], 0.0)
                for j in range(LOG2_CHUNK):
                    a = jnp.where(level_masks[j], prods[1 + j], a)
            intra.append((a.astype(BF16), _block_diag_pair(ops["v"][:, pair])))
        inter, new_state = [], []
        for h in range(heads):
            sl = slice(h * GROUP, (h + 1) * GROUP)
            s_t = state[h]
            inter.append(_dot_nt(ops["q_dec"][:, sl], s_t.astype(BF16)))
            new_state.append(ops["e_last"][:, sl] * s_t + _dot_tn(ops["v"][:, sl], ops["k_dec"][:, sl]))
        mid = mid_mxu_work()
        for p, (a, v_bd) in enumerate(intra):
            o_pair = _dot(a, v_bd)
            for i in range(2):
                h = 2 * p + i
                sl = slice(h * GROUP, (h + 1) * GROUP)
                o = inter[h] + o_pair[:, i * GROUP:(i + 1) * GROUP]
                y_ref[rows, sl] = (_rms_scale(o) * onorm_g_ref[:, sl] * ops["gate"][:, sl]).astype(y_ref.dtype)
        return new_state, mid

    def run_tile(stage3):
        state = [state_ref[h] for h in range(heads)]
        g = {c: gates(c) for c in range(min(2, n_chunks))}
        b = {c: cumulative_decay(g[c]) for c in g}
        ops = stage3(g[0], b[0])
        for c in range(n_chunks):
            if c + 2 < n_chunks:
                g[c + 2] = gates(c + 2)
            nxt = stage3(g[c + 1], b[c + 1]) if c + 1 < n_chunks else None
            mid = (lambda c=c: cumulative_decay(g[c + 2])) if c + 2 < n_chunks else (lambda: None)
            state, b[c + 2] = recur(c, ops, state, mid)
            ops = nxt
        for h in range(heads):
            state_ref[h] = state[h]

    gates_bounded = jnp.min(lb) >= MIN_GATE_FOR_SINGLE_PRODUCT

    @pl.when(gates_bounded)
    def _():
        run_tile(scaled_operands_bounded)

    @pl.when(jnp.logical_not(gates_bounded))
    def _():
        run_tile(scaled_operands)


def _output_kernel(x_ref, yc_ref, yh_ref, p_ref, w_out_c_ref, w_out_h_ref, pe_g_ref, w_pg_ref,
                   w_pp_ref, final_g_ref, out_ref):
    tile = x_ref.shape[0]
    part = tile // OUTPUT_ROW_PARTS
    halves = [slice(i * part, (i + 1) * part) for i in range(OUTPUT_ROW_PARTS)]
    hs = [x_ref[rows, :] + _dot(yc_ref[rows, :], w_out_c_ref[...]) + _dot(yh_ref[rows, :], w_out_h_ref[...])
          for rows in halves]
    pes = [_dot(p_ref[rows, :].astype(BF16), w_pp_ref[...]) for rows in halves]
    for rows, h, pe in zip(halves, hs, pes):
        gate = _sigmoid(_dot((_rms_scale(h) * pe_g_ref[...]).astype(BF16), w_pg_ref[...]))
        h = h + gate * pe
        out_ref[rows, :] = _rms_scale(h) * final_g_ref[...]


def _full(shape):
    return pl.BlockSpec(shape, lambda b, t: (0,) * len(shape))


def _layer(x, p_i, ln_g, w_in, conv_w, conv_b, cn_g, cn_b, w_pw2, b_pw2, lb_logits, onorm_g,
           w_out, pe_g, w_pg, w_pp, out_g, layer):
    batch, seq, d_model = x.shape
    width = conv_w.shape[-1]
    ple = p_i.shape[-1]
    tile = min(TIME_TILE, seq)
    assert seq % tile == 0 and tile % CHUNK == 0 and width % (2 * GROUP) == 0
    assert tile % (2 * SUBLANES * CONV_UNROLL) == 0 and width % CONV_LANES == 0
    grid = (batch, seq // tile)
    row2 = lambda a: a.reshape(1, -1)
    tok = lambda w: pl.BlockSpec((None, tile, w), lambda b, t: (b, t, 0))
    params = pltpu.CompilerParams(dimension_semantics=("arbitrary", "arbitrary"),
                                  vmem_limit_bytes=VMEM_LIMIT_BYTES)

    w_in_c = w_in[:, :3 * width].astype(BF16)
    w_in_h = w_in[:, 3 * width:].astype(BF16)

    y_conv = pl.pallas_call(
        functools.partial(_conv_branch_kernel, tile=tile, width=width),
        grid=grid,
        in_specs=[tok(d_model), _full((1, d_model)), _full((d_model, 3 * width)),
                  _full((CONV_K, width)), _full((1, width)), _full((1, width)), _full((1, width)),
                  _full((width, width)), _full((1, width))],
        out_specs=tok(width),
        out_shape=jax.ShapeDtypeStruct((batch, seq, width), BF16),
        scratch_shapes=[pltpu.VMEM((HALO + tile, width), F32), pltpu.VMEM((tile, width), F32),
                        pltpu.VMEM((SUBLANES * CONV_K, width), F32)],
        compiler_params=params,
        name="conv_branch",
    )(x, row2(ln_g), w_in_c, conv_w, row2(conv_b), row2(cn_g), row2(cn_b),
      w_pw2.astype(BF16), row2(b_pw2))

    tril = np.tril(np.ones((CHUNK, CHUNK), np.float32))
    tri = jnp.asarray(np.concatenate([tril, tril], axis=1), BF16)
    y_hgrn = pl.pallas_call(
        functools.partial(_hgrn_branch_kernel, tile=tile, width=width, layer=layer),
        grid=grid,
        in_specs=[tok(d_model), _full((1, d_model)), _full((d_model, 4 * width)),
                  _full(lb_logits.shape), _full((1, width)), _full(tri.shape)],
        out_specs=tok(width),
        out_shape=jax.ShapeDtypeStruct((batch, seq, width), BF16),
        scratch_shapes=[pltpu.VMEM((width // GROUP, GROUP, GROUP), F32)],
        compiler_params=params,
        name="hgrn_branch",
    )(x, row2(ln_g), w_in_h, lb_logits, row2(onorm_g), tri)

    out_tile = min(OUTPUT_TILE, seq)
    assert seq % out_tile == 0 and out_tile % (OUTPUT_ROW_PARTS * SUBLANES) == 0
    otok = lambda w: pl.BlockSpec((None, out_tile, w), lambda b, t: (b, t, 0))
    return pl.pallas_call(
        _output_kernel,
        grid=(batch, seq // out_tile),
        in_specs=[otok(d_model), otok(width), otok(width), otok(ple),
                  _full((width, d_model)), _full((width, d_model)), _full((1, d_model)),
                  _full((d_model, d_model)), _full((ple, d_model)), _full((1, d_model))],
        out_specs=otok(d_model),
        out_shape=jax.ShapeDtypeStruct((batch, seq, d_model), x.dtype),
        compiler_params=params,
        name="output_proj",
    )(x, y_conv, y_hgrn, p_i, w_out[:width].astype(BF16), w_out[width:].astype(BF16),
      row2(pe_g), w_pg.astype(BF16), w_pp.astype(BF16), row2(out_g))


def kernel(x, p, ln_g, w_in, conv_w, conv_b, cnorm_g, cnorm_b, w_pw2, b_pw2, lb_logits, onorm_g,
           w_out, pe_norm_g, w_pg, w_pp, final_g):
    depth = p.shape[0]
    assert depth == 1, "the final norm is fused into the only layer's output kernel"
    return _layer(x, p[0], ln_g[0], w_in[0], conv_w[0], conv_b[0], cnorm_g[0], cnorm_b[0],
                  w_pw2[0], b_pw2[0], lb_logits, onorm_g[0], w_out[0], pe_norm_g[0], w_pg[0],
                  w_pp[0], final_g, layer=0)
```

```python
import functools

import numpy as np
import jax
import jax.numpy as jnp
from jax import lax
from jax.experimental import pallas as pl
from jax.experimental.pallas import tpu as pltpu

EPS = 1e-6
CONV_K = 31
GROUP = 128
CHUNK = 64
LOG2_CHUNK = 6
SUBLANES = 8
HALO = 32
CONV_LANES = 256
CONV_UNROLL = 4
MIN_GATE_FOR_SINGLE_PRODUCT = 0.125
CONV_TILE = 1024
CONV_ROW_PARTS = 4
OUTPUT_ROW_PARTS = 2
OUTPUT_TILE = 1024
TIME_TILE = 512
VMEM_LIMIT_BYTES = 56 * 1024 * 1024

HIGH_HALF = np.uint32(0xFFFF0000)

BF16 = jnp.bfloat16
F32 = jnp.float32


def _rms_scale(x):
    return x * lax.rsqrt(jnp.mean(x * x, axis=-1, keepdims=True) + EPS)


def _sigmoid(x):
    return jax.nn.sigmoid(x)


def _dot(a, b):
    return jnp.dot(a, b, preferred_element_type=F32)


def _dot_nt(a, b):
    return lax.dot_general(a, b, (((1,), (1,)), ((), ())), preferred_element_type=F32)


def _dot_tn(a, b):
    return lax.dot_general(a, b, (((0,), (0,)), ((), ())), preferred_element_type=F32)


def _conv_branch_kernel(x_ref, ln_g_ref, w_in_ref, conv_w_ref, conv_b_ref, cn_g_ref, cn_b_ref,
                        w_pw2_ref, b_pw2_ref, y_ref, vbuf_ref, conv_ref, wb_ref, *, tile, width):
    t_idx = pl.program_id(1)
    part = tile // CONV_ROW_PARTS
    halves = [slice(i * part, (i + 1) * part) for i in range(CONV_ROW_PARTS)]

    @pl.when(t_idx == 0)
    def _():
        vbuf_ref[0:HALO, :] = jnp.zeros((HALO, width), F32)

    @pl.when((pl.program_id(0) == 0) & (t_idx == 0))
    def _():
        for j in range(CONV_K):
            wb_ref[SUBLANES * j:SUBLANES * (j + 1), :] = jnp.broadcast_to(
                conv_w_ref[j:j + 1, :], (SUBLANES, width))

    z_gate = []
    for rows in halves:
        u = (_rms_scale(x_ref[rows, :]) * ln_g_ref[...]).astype(BF16)
        z = _dot(u, w_in_ref[...])
        vbuf_ref[HALO + rows.start:HALO + rows.stop, :] = z[:, :width] * _sigmoid(z[:, width:2 * width])
        z_gate.append(z[:, 2 * width:])

    sub = lax.broadcasted_iota(jnp.int32, (SUBLANES, CONV_LANES), 0)
    for lt in range(width // CONV_LANES):
        lanes = slice(lt * CONV_LANES, (lt + 1) * CONV_LANES)
        bias = jnp.broadcast_to(conv_b_ref[:, lanes], (SUBLANES, CONV_LANES))

        def partial_sums(row):
            blocks = [vbuf_ref[pl.ds(HALO + row - SUBLANES * a, SUBLANES), lanes]
                      for a in range(HALO // SUBLANES)]
            sums = []
            for r in range(SUBLANES):
                g = None
                for a, blk in enumerate(blocks):
                    s = SUBLANES * a + r
                    if s < CONV_K:
                        j = CONV_K - 1 - s
                        term = blk * wb_ref[SUBLANES * j:SUBLANES * (j + 1), lanes]
                        g = term if g is None else g + term
                sums.append(g)
            return sums

        def conv_rows(i, prev):
            r0 = pl.multiple_of(i * SUBLANES, SUBLANES)
            cur = partial_sums(r0)
            y = cur[0] + bias
            for r in range(1, SUBLANES):
                y = y + pltpu.roll(jnp.where(sub < SUBLANES - r, cur[r], prev[r - 1]), r, axis=0)
            conv_ref[pl.ds(r0, SUBLANES), lanes] = y
            return tuple(cur[1:])

        lax.fori_loop(0, tile // SUBLANES, conv_rows, tuple(partial_sums(-SUBLANES)[1:]),
                      unroll=CONV_UNROLL)

    vbuf_ref[0:HALO, :] = vbuf_ref[tile:tile + HALO, :]

    for rows, zg in zip(halves, z_gate):
        normed = []
        for g in range(width // GROUP):
            sl = slice(g * GROUP, (g + 1) * GROUP)
            yg = conv_ref[rows, sl]
            d = yg - jnp.mean(yg, axis=-1, keepdims=True)
            yn = d * lax.rsqrt(jnp.mean(d * d, axis=-1, keepdims=True) + EPS)
            yn = yn * cn_g_ref[:, sl] + cn_b_ref[:, sl]
            normed.append((yn * _sigmoid(yn)).astype(BF16))
        y = _dot(jnp.concatenate(normed, axis=1), w_pw2_ref[...]) + b_pw2_ref[...]
        y_ref[rows, :] = (y * (zg * _sigmoid(zg))).astype(y_ref.dtype)


def _boundary_rows(b, j):
    half = 1 << j
    rows, width = b.shape
    if 2 * half >= SUBLANES:
        return jnp.concatenate(
            [jnp.broadcast_to(b[s + half - 1:s + half, :], (2 * half, width))
             for s in range(0, rows, 2 * half)], axis=0)
    assert 4 * half == SUBLANES
    sub = lax.broadcasted_iota(jnp.int32, (SUBLANES, width), 0)
    return jnp.concatenate(
        [jnp.where(sub < 2 * half,
                   jnp.broadcast_to(b[s + half - 1:s + half, :], (SUBLANES, width)),
                   jnp.broadcast_to(b[s + 3 * half - 1:s + 3 * half, :], (SUBLANES, width)))
         for s in range(0, rows, SUBLANES)], axis=0)


def _block_diag_pair(m):
    zero = jnp.zeros((m.shape[0], GROUP), m.dtype)
    return jnp.concatenate([jnp.concatenate([m[:, :GROUP], zero], axis=1),
                            jnp.concatenate([zero, m[:, GROUP:]], axis=1)], axis=0)


def _hgrn_branch_kernel(x_ref, ln_g_ref, w_in_ref, lb_logits_ref, onorm_g_ref, tri_ref, y_ref,
                        state_ref, *, tile, width, layer):
    heads = width // GROUP
    n_chunks = tile // CHUNK

    @pl.when(pl.program_id(1) == 0)
    def _():
        state_ref[...] = jnp.zeros(state_ref.shape, F32)

    z = jnp.concatenate(
        [_dot((_rms_scale(x_ref[rows, :]) * ln_g_ref[...]).astype(BF16), w_in_ref[...])
         for rows in (slice(0, tile // 2), slice(tile // 2, tile))], axis=0)

    logits = lb_logits_ref[...]
    e = jnp.exp(logits - jnp.max(logits, axis=0, keepdims=True))
    lb = jnp.sum(e[:layer + 1], axis=0, keepdims=True) / jnp.sum(e, axis=0, keepdims=True)

    row = lax.broadcasted_iota(jnp.int32, (CHUNK, 2 * CHUNK), 0)
    col = lax.broadcasted_iota(jnp.int32, (CHUNK, 2 * CHUNK), 1) & (CHUNK - 1)
    row_xor_col = row ^ col
    causal = row > col
    level_masks = [causal & ((row_xor_col >> j) == 1) for j in range(LOG2_CHUNK)]
    diag_mask = row == col
    odd_row = (lax.broadcasted_iota(jnp.int32, (CHUNK, width), 0) & 1) == 1
    tri_twice = tri_ref[...]

    def gates(c):
        rows = slice(c * CHUNK, (c + 1) * CHUNK)
        zq = z[rows, :width]
        sig_f = _sigmoid(z[rows, width:2 * width])
        zg = z[rows, 3 * width:]
        lf = jnp.log(lb + (1.0 - lb) * sig_f)
        lf_top = lax.bitcast_convert_type(lax.bitcast_convert_type(lf, jnp.uint32) & HIGH_HALF, F32)
        lf_pieces = jnp.concatenate([lf_top.astype(BF16), (lf - lf_top).astype(BF16)], axis=0)
        return dict(q=zq * _sigmoid(zq),
                    k=(1.0 - lb) * (1.0 - sig_f),
                    lf=lf, lf_pieces=lf_pieces,
                    v=z[rows, 2 * width:3 * width].astype(BF16), gate=zg * _sigmoid(zg))

    def cumulative_decay(g):
        return _dot(tri_twice, g["lf_pieces"])

    def scaled_operands_bounded(g, b):
        q, k = g["q"], g["k"]
        b_mid = b[CHUNK // 2 - 1:CHUNK // 2, :]
        b_last = b[CHUNK - 1:CHUNK, :]
        q_mid = q * jnp.exp(b - b_mid)
        k_mid = k * jnp.exp(b_mid - b)
        return dict(q_lvl=[q_mid.astype(BF16)], k_lvl=[k_mid.astype(BF16)],
                    q_dec=(q_mid * jnp.exp(b_mid)).astype(BF16),
                    k_dec=(k_mid * jnp.exp(b_last - b_mid)).astype(BF16), e_last=jnp.exp(b_last),
                    v=g["v"], gate=g["gate"])

    def scaled_operands(g, b):
        q, k, lf = g["q"], g["k"], g["lf"]
        e_b = jnp.exp(b)
        b_last = b[CHUNK - 1:CHUNK, :]
        factors = [jnp.where(odd_row, jnp.exp(lf), 1.0)]
        for j in range(1, LOG2_CHUNK):
            factors.append(jnp.exp(-jnp.abs(b - _boundary_rows(b, j))))
        q16, k16 = q.astype(BF16), k.astype(BF16)
        q_lvl, k_lvl = [q16], [k16]
        for w in factors:
            w16 = w.astype(BF16)
            q_lvl.append(q16 * w16)
            k_lvl.append(k16 * w16)
        return dict(q_lvl=q_lvl, k_lvl=k_lvl, q_dec=(q * e_b).astype(BF16),
                    k_dec=(k * jnp.exp(b_last - b)).astype(BF16), e_last=jnp.exp(b_last),
                    v=g["v"], gate=g["gate"])

    def recur(c, ops, state, mid_mxu_work):
        rows = slice(c * CHUNK, (c + 1) * CHUNK)
        intra = []
        for p in range(heads // 2):
            pair = slice(2 * p * GROUP, 2 * (p + 1) * GROUP)
            prods = [_dot_nt(ql[:, pair], _block_diag_pair(kl[:, pair]))
                     for ql, kl in zip(ops["q_lvl"], ops["k_lvl"])]
            if len(prods) == 1:
                a = jnp.where(causal | diag_mask, prods[0], 0.0)
            else:
                a = jnp.where(diag_mask, prods[0], 0.0)
                for j in range(LOG2_CHUNK):
                    a = jnp.where(level_masks[j], prods[1 + j], a)
            intra.append((a.astype(BF16), _block_diag_pair(ops["v"][:, pair])))
        inter, new_state = [], []
        for h in range(heads):
            sl = slice(h * GROUP, (h + 1) * GROUP)
            s_t = state[h]
            inter.append(_dot_nt(ops["q_dec"][:, sl], s_t.astype(BF16)))
            new_state.append(ops["e_last"][:, sl] * s_t + _dot_tn(ops["v"][:, sl], ops["k_dec"][:, sl]))
        mid = mid_mxu_work()
        for p, (a, v_bd) in enumerate(intra):
            o_pair = _dot(a, v_bd)
            for i in range(2):
                h = 2 * p + i
                sl = slice(h * GROUP, (h + 1) * GROUP)
                o = inter[h] + o_pair[:, i * GROUP:(i + 1) * GROUP]
                y_ref[rows, sl] = (_rms_scale(o) * onorm_g_ref[:, sl] * ops["gate"][:, sl]).astype(y_ref.dtype)
        return new_state, mid

    def run_tile(stage3):
        state = [state_ref[h] for h in range(heads)]
        g = {c: gates(c) for c in range(min(2, n_chunks))}
        b = {c: cumulative_decay(g[c]) for c in g}
        ops = stage3(g[0], b[0])
        for c in range(n_chunks):
            if c + 2 < n_chunks:
                g[c + 2] = gates(c + 2)
            nxt = stage3(g[c + 1], b[c + 1]) if c + 1 < n_chunks else None
            mid = (lambda c=c: cumulative_decay(g[c + 2])) if c + 2 < n_chunks else (lambda: None)
            state, b[c + 2] = recur(c, ops, state, mid)
            ops = nxt
        for h in range(heads):
            state_ref[h] = state[h]

    gates_bounded = jnp.min(lb) >= MIN_GATE_FOR_SINGLE_PRODUCT

    @pl.when(gates_bounded)
    def _():
        run_tile(scaled_operands_bounded)

    @pl.when(jnp.logical_not(gates_bounded))
    def _():
        run_tile(scaled_operands)


def _output_kernel(x_ref, yc_ref, yh_ref, p_ref, w_out_c_ref, w_out_h_ref, pe_g_ref, w_pg_ref,
                   w_pp_ref, final_g_ref, out_ref):
    tile = x_ref.shape[0]
    part = tile // OUTPUT_ROW_PARTS
    halves = [slice(i * part, (i + 1) * part) for i in range(OUTPUT_ROW_PARTS)]
    hs = [x_ref[rows, :] + _dot(yc_ref[rows, :], w_out_c_ref[...]) + _dot(yh_ref[rows, :], w_out_h_ref[...])
          for rows in halves]
    pes = [_dot(p_ref[rows, :].astype(BF16), w_pp_ref[...]) for rows in halves]
    for rows, h, pe in zip(halves, hs, pes):
        gate = _sigmoid(_dot((_rms_scale(h) * pe_g_ref[...]).astype(BF16), w_pg_ref[...]))
        h = h + gate * pe
        out_ref[rows, :] = _rms_scale(h) * final_g_ref[...]


def _full(shape):
    return pl.BlockSpec(shape, lambda b, t: (0,) * len(shape))


def _layer(x, p_i, ln_g, w_in, conv_w, conv_b, cn_g, cn_b, w_pw2, b_pw2, lb_logits, onorm_g,
           w_out, pe_g, w_pg, w_pp, out_g, layer):
    batch, seq, d_model = x.shape
    width = conv_w.shape[-1]
    ple = p_i.shape[-1]
    tile = min(TIME_TILE, seq)
    assert seq % tile == 0 and tile % CHUNK == 0 and width % (2 * GROUP) == 0
    assert tile % (2 * SUBLANES * CONV_UNROLL) == 0 and width % CONV_LANES == 0
    grid = (batch, seq // tile)
    row2 = lambda a: a.reshape(1, -1)
    tok = lambda w: pl.BlockSpec((None, tile, w), lambda b, t: (b, t, 0))
    params = pltpu.CompilerParams(dimension_semantics=("arbitrary", "arbitrary"),
                                  vmem_limit_bytes=VMEM_LIMIT_BYTES)

    w_in_c = w_in[:, :3 * width].astype(BF16)
    w_in_h = w_in[:, 3 * width:].astype(BF16)

    conv_tile = min(CONV_TILE, seq)
    assert seq % conv_tile == 0 and conv_tile % (CONV_ROW_PARTS * SUBLANES * CONV_UNROLL) == 0
    ctok = lambda w: pl.BlockSpec((None, conv_tile, w), lambda b, t: (b, t, 0))
    y_conv = pl.pallas_call(
        functools.partial(_conv_branch_kernel, tile=conv_tile, width=width),
        grid=(batch, seq // conv_tile),
        in_specs=[ctok(d_model), _full((1, d_model)), _full((d_model, 3 * width)),
                  _full((CONV_K, width)), _full((1, width)), _full((1, width)), _full((1, width)),
                  _full((width, width)), _full((1, width))],
        out_specs=ctok(width),
        out_shape=jax.ShapeDtypeStruct((batch, seq, width), BF16),
        scratch_shapes=[pltpu.VMEM((HALO + conv_tile, width), F32), pltpu.VMEM((conv_tile, width), F32),
                        pltpu.VMEM((SUBLANES * CONV_K, width), F32)],
        compiler_params=params,
        name="conv_branch",
    )(x, row2(ln_g), w_in_c, conv_w, row2(conv_b), row2(cn_g), row2(cn_b),
      w_pw2.astype(BF16), row2(b_pw2))

    tril = np.tril(np.ones((CHUNK, CHUNK), np.float32))
    tri = jnp.asarray(np.concatenate([tril, tril], axis=1), BF16)
    y_hgrn = pl.pallas_call(
        functools.partial(_hgrn_branch_kernel, tile=tile, width=width, layer=layer),
        grid=grid,
        in_specs=[tok(d_model), _full((1, d_model)), _full((d_model, 4 * width)),
                  _full(lb_logits.shape), _full((1, width)), _full(tri.shape)],
        out_specs=tok(width),
        out_shape=jax.ShapeDtypeStruct((batch, seq, width), BF16),
        scratch_shapes=[pltpu.VMEM((width // GROUP, GROUP, GROUP), F32)],
        compiler_params=params,
        name="hgrn_branch",
    )(x, row2(ln_g), w_in_h, lb_logits, row2(onorm_g), tri)

    out_tile = min(OUTPUT_TILE, seq)
    assert seq % out_tile == 0 and out_tile % (OUTPUT_ROW_PARTS * SUBLANES) == 0
    otok = lambda w: pl.BlockSpec((None, out_tile, w), lambda b, t: (b, t, 0))
    return pl.pallas_call(
        _output_kernel,
        grid=(batch, seq // out_tile),
        in_specs=[otok(d_model), otok(width), otok(width), otok(ple),
                  _full((width, d_model)), _full((width, d_model)), _full((1, d_model)),
                  _full((d_model, d_model)), _full((ple, d_model)), _full((1, d_model))],
        out_specs=otok(d_model),
        out_shape=jax.ShapeDtypeStruct((batch, seq, d_model), x.dtype),
        compiler_params=params,
        name="output_proj",
    )(x, y_conv, y_hgrn, p_i, w_out[:width].astype(BF16), w_out[width:].astype(BF16),
      row2(pe_g), w_pg.astype(BF16), w_pp.astype(BF16), row2(out_g))


def kernel(x, p, ln_g, w_in, conv_w, conv_b, cnorm_g, cnorm_b, w_pw2, b_pw2, lb_logits, onorm_g,
           w_out, pe_norm_g, w_pg, w_pp, final_g):
    depth = p.shape[0]
    assert depth == 1, "the final norm is fused into the only layer's output kernel"
    return _layer(x, p[0], ln_g[0], w_in[0], conv_w[0], conv_b[0], cnorm_g[0], cnorm_b[0],
                  w_pw2[0], b_pw2[0], lb_logits, onorm_g[0], w_out[0], pe_norm_g[0], w_pg[0],
                  w_pp[0], final_g, layer=0)
```

```python
import functools

import numpy as np
import jax
import jax.numpy as jnp
from jax import lax
from jax.experimental import pallas as pl
from jax.experimental.pallas import tpu as pltpu

EPS = 1e-6
CONV_K = 31
GROUP = 128
CHUNK = 64
LOG2_CHUNK = 6
SUBLANES = 8
HALO = 32
CONV_LANES = 256
CONV_UNROLL = 4
MIN_GATE_FOR_SINGLE_PRODUCT = 0.125
CONV_TILE = 1024
CONV_ROW_PARTS = 4
OUTPUT_ROW_PARTS = 2
OUTPUT_TILE = 1024
TIME_TILE = 512
VMEM_LIMIT_BYTES = 56 * 1024 * 1024

HIGH_HALF = np.uint32(0xFFFF0000)

BF16 = jnp.bfloat16
F32 = jnp.float32


def _rms_scale(x):
    return x * lax.rsqrt(jnp.mean(x * x, axis=-1, keepdims=True) + EPS)


def _sigmoid(x):
    return jax.nn.sigmoid(x)


def _dot(a, b):
    return jnp.dot(a, b, preferred_element_type=F32)


def _dot_nt(a, b):
    return lax.dot_general(a, b, (((1,), (1,)), ((), ())), preferred_element_type=F32)


def _dot_tn(a, b):
    return lax.dot_general(a, b, (((0,), (0,)), ((), ())), preferred_element_type=F32)


def _conv_branch_kernel(x_ref, ln_g_ref, w_in_ref, conv_w_ref, conv_b_ref, cn_g_ref, cn_b_ref,
                        w_pw2_ref, b_pw2_ref, y_ref, vbuf_ref, conv_ref, wb_ref, *, tile, width):
    t_idx = pl.program_id(1)
    part = tile // CONV_ROW_PARTS
    halves = [slice(i * part, (i + 1) * part) for i in range(CONV_ROW_PARTS)]

    @pl.when(t_idx == 0)
    def _():
        vbuf_ref[0:HALO, :] = jnp.zeros((HALO, width), F32)

    @pl.when((pl.program_id(0) == 0) & (t_idx == 0))
    def _():
        for j in range(CONV_K):
            wb_ref[SUBLANES * j:SUBLANES * (j + 1), :] = jnp.broadcast_to(
                conv_w_ref[j:j + 1, :], (SUBLANES, width))

    z_gate = []
    for rows in halves:
        u = (_rms_scale(x_ref[rows, :]) * ln_g_ref[...]).astype(BF16)
        z = _dot(u, w_in_ref[...])
        vbuf_ref[HALO + rows.start:HALO + rows.stop, :] = z[:, :width] * _sigmoid(z[:, width:2 * width])
        z_gate.append(z[:, 2 * width:])

    sub = lax.broadcasted_iota(jnp.int32, (SUBLANES, CONV_LANES), 0)
    for lt in range(width // CONV_LANES):
        lanes = slice(lt * CONV_LANES, (lt + 1) * CONV_LANES)
        bias = jnp.broadcast_to(conv_b_ref[:, lanes], (SUBLANES, CONV_LANES))

        def partial_sums(row):
            blocks = [vbuf_ref[pl.ds(HALO + row - SUBLANES * a, SUBLANES), lanes]
                      for a in range(HALO // SUBLANES)]
            sums = []
            for r in range(SUBLANES):
                g = None
                for a, blk in enumerate(blocks):
                    s = SUBLANES * a + r
                    if s < CONV_K:
                        j = CONV_K - 1 - s
                        term = blk * wb_ref[SUBLANES * j:SUBLANES * (j + 1), lanes]
                        g = term if g is None else g + term
                sums.append(g)
            return sums

        def conv_rows(i, prev):
            r0 = pl.multiple_of(i * SUBLANES, SUBLANES)
            cur = partial_sums(r0)
            y = cur[0] + bias
            for r in range(1, SUBLANES):
                y = y + pltpu.roll(jnp.where(sub < SUBLANES - r, cur[r], prev[r - 1]), r, axis=0)
            conv_ref[pl.ds(r0, SUBLANES), lanes] = y
            return tuple(cur[1:])

        lax.fori_loop(0, tile // SUBLANES, conv_rows, tuple(partial_sums(-SUBLANES)[1:]),
                      unroll=CONV_UNROLL)

    vbuf_ref[0:HALO, :] = vbuf_ref[tile:tile + HALO, :]

    for rows, zg in zip(halves, z_gate):
        normed = []
        for g in range(width // GROUP):
            sl = slice(g * GROUP, (g + 1) * GROUP)
            yg = conv_ref[rows, sl]
            d = yg - jnp.mean(yg, axis=-1, keepdims=True)
            yn = d * lax.rsqrt(jnp.mean(d * d, axis=-1, keepdims=True) + EPS)
            yn = yn * cn_g_ref[:, sl] + cn_b_ref[:, sl]
            normed.append((yn * _sigmoid(yn)).astype(BF16))
        y = _dot(jnp.concatenate(normed, axis=1), w_pw2_ref[...]) + b_pw2_ref[...]
        y_ref[rows, :] = (y * (zg * _sigmoid(zg))).astype(y_ref.dtype)


def _boundary_rows(b, j):
    half = 1 << j
    rows, width = b.shape
    if 2 * half >= SUBLANES:
        return jnp.concatenate(
            [jnp.broadcast_to(b[s + half - 1:s + half, :], (2 * half, width))
             for s in range(0, rows, 2 * half)], axis=0)
    assert 4 * half == SUBLANES
    sub = lax.broadcasted_iota(jnp.int32, (SUBLANES, width), 0)
    return jnp.concatenate(
        [jnp.where(sub < 2 * half,
                   jnp.broadcast_to(b[s + half - 1:s + half, :], (SUBLANES, width)),
                   jnp.broadcast_to(b[s + 3 * half - 1:s + 3 * half, :], (SUBLANES, width)))
         for s in range(0, rows, SUBLANES)], axis=0)


def _block_diag_pair(m):
    zero = jnp.zeros((m.shape[0], GROUP), m.dtype)
    return jnp.concatenate([jnp.concatenate([m[:, :GROUP], zero], axis=1),
                            jnp.concatenate([zero, m[:, GROUP:]], axis=1)], axis=0)


def _hgrn_branch_kernel(x_ref, ln_g_ref, w_in_ref, lb_logits_ref, onorm_g_ref, tri_ref, y_ref,
                        state_ref, *, tile, width, layer):
    heads = width // GROUP
    n_chunks = tile // CHUNK

    @pl.when(pl.program_id(1) == 0)
    def _():
        state_ref[...] = jnp.zeros(state_ref.shape, F32)

    z = jnp.concatenate(
        [_dot((_rms_scale(x_ref[rows, :]) * ln_g_ref[...]).astype(BF16), w_in_ref[...])
         for rows in (slice(0, tile // 2), slice(tile // 2, tile))], axis=0)

    logits = lb_logits_ref[...]
    e = jnp.exp(logits - jnp.max(logits, axis=0, keepdims=True))
    lb = jnp.sum(e[:layer + 1], axis=0, keepdims=True) / jnp.sum(e, axis=0, keepdims=True)

    row = lax.broadcasted_iota(jnp.int32, (CHUNK, 2 * CHUNK), 0)
    col = lax.broadcasted_iota(jnp.int32, (CHUNK, 2 * CHUNK), 1) & (CHUNK - 1)
    row_xor_col = row ^ col
    causal = row > col
    level_masks = [causal & ((row_xor_col >> j) == 1) for j in range(LOG2_CHUNK)]
    diag_mask = row == col
    odd_row = (lax.broadcasted_iota(jnp.int32, (CHUNK, width), 0) & 1) == 1
    tri_twice = tri_ref[...]

    def gates(c):
        rows = slice(c * CHUNK, (c + 1) * CHUNK)
        zq = z[rows, :width]
        sig_f = _sigmoid(z[rows, width:2 * width])
        zg = z[rows, 3 * width:]
        lf = jnp.log(lb + (1.0 - lb) * sig_f)
        lf_top = lax.bitcast_convert_type(lax.bitcast_convert_type(lf, jnp.uint32) & HIGH_HALF, F32)
        lf_pieces = jnp.concatenate([lf_top.astype(BF16), (lf - lf_top).astype(BF16)], axis=0)
        return dict(q=zq * _sigmoid(zq),
                    k=(1.0 - lb) * (1.0 - sig_f),
                    lf=lf, lf_pieces=lf_pieces,
                    v=z[rows, 2 * width:3 * width].astype(BF16), gate=zg * _sigmoid(zg))

    def cumulative_decay(g):
        return _dot(tri_twice, g["lf_pieces"])

    def scaled_operands_bounded(g, b):
        q, k = g["q"], g["k"]
        b_mid = b[CHUNK // 2 - 1:CHUNK // 2, :]
        b_last = b[CHUNK - 1:CHUNK, :]
        q_mid = q * jnp.exp(b - b_mid)
        k_mid = k * jnp.exp(b_mid - b)
        return dict(q_lvl=[q_mid.astype(BF16)], k_lvl=[k_mid.astype(BF16)],
                    q_dec=(q_mid * jnp.exp(b_mid)).astype(BF16),
                    k_dec=(k_mid * jnp.exp(b_last - b_mid)).astype(BF16), e_last=jnp.exp(b_last),
                    v=g["v"], gate=g["gate"])

    def scaled_operands(g, b):
        q, k, lf = g["q"], g["k"], g["lf"]
        e_b = jnp.exp(b)
        b_last = b[CHUNK - 1:CHUNK, :]
        factors = [jnp.where(odd_row, jnp.exp(lf), 1.0)]
        for j in range(1, LOG2_CHUNK):
            factors.append(jnp.exp(-jnp.abs(b - _boundary_rows(b, j))))
        q16, k16 = q.astype(BF16), k.astype(BF16)
        q_lvl, k_lvl = [q16], [k16]
        for w in factors:
            w16 = w.astype(BF16)
            q_lvl.append(q16 * w16)
            k_lvl.append(k16 * w16)
        return dict(q_lvl=q_lvl, k_lvl=k_lvl, q_dec=(q * e_b).astype(BF16),
                    k_dec=(k * jnp.exp(b_last - b)).astype(BF16), e_last=jnp.exp(b_last),
                    v=g["v"], gate=g["gate"])

    def recur(c, ops, state, mid_mxu_work):
        rows = slice(c * CHUNK, (c + 1) * CHUNK)
        intra = []
        for p in range(heads // 2):
            pair = slice(2 * p * GROUP, 2 * (p + 1) * GROUP)
            prods = [_dot_nt(ql[:, pair], _block_diag_pair(kl[:, pair]))
                     for ql, kl in zip(ops["q_lvl"], ops["k_lvl"])]
            if len(prods) == 1:
                a = jnp.where(causal | diag_mask, prods[0], 0.0)
            else:
                a = jnp.where(diag_mask, prods[0], 0.0)
                for j in range(LOG2_CHUNK):
                    a = jnp.where(level_masks[j], prods[1 + j], a)
            intra.append((a.astype(BF16), _block_diag_pair(ops["v"][:, pair])))
        inter, new_state = [], []
        for h in range(heads):
            sl = slice(h * GROUP, (h + 1) * GROUP)
            s_t = state[h]
            inter.append(_dot_nt(ops["q_dec"][:, sl], s_t.astype(BF16)))
            new_state.append(ops["e_last"][:, sl] * s_t + _dot_tn(ops["v"][:, sl], ops["k_dec"][:, sl]))
        mid = mid_mxu_work()
        for p, (a, v_bd) in enumerate(intra):
            o_pair = _dot(a, v_bd)
            for i in range(2):
                h = 2 * p + i
                sl = slice(h * GROUP, (h + 1) * GROUP)
                o = inter[h] + o_pair[:, i * GROUP:(i + 1) * GROUP]
                y_ref[rows, sl] = (_rms_scale(o) * onorm_g_ref[:, sl] * ops["gate"][:, sl]).astype(y_ref.dtype)
        return new_state, mid

    def run_tile(stage3):
        state = [state_ref[h] for h in range(heads)]
        g = {0: gates(0)}
        ops = stage3(g[0], cumulative_decay(g[0]))
        for c in range(n_chunks):
            if c + 1 < n_chunks:
                g[c + 1] = gates(c + 1)
            mid = (lambda c=c: cumulative_decay(g[c + 1])) if c + 1 < n_chunks else (lambda: None)
            state, b_next = recur(c, ops, state, mid)
            ops = stage3(g.pop(c + 1), b_next) if c + 1 < n_chunks else None
        for h in range(heads):
            state_ref[h] = state[h]

    gates_bounded = jnp.min(lb) >= MIN_GATE_FOR_SINGLE_PRODUCT

    @pl.when(gates_bounded)
    def _():
        run_tile(scaled_operands_bounded)

    @pl.when(jnp.logical_not(gates_bounded))
    def _():
        run_tile(scaled_operands)


def _output_kernel(x_ref, yc_ref, yh_ref, p_ref, w_out_c_ref, w_out_h_ref, pe_g_ref, w_pg_ref,
                   w_pp_ref, final_g_ref, out_ref):
    tile = x_ref.shape[0]
    part = tile // OUTPUT_ROW_PARTS
    halves = [slice(i * part, (i + 1) * part) for i in range(OUTPUT_ROW_PARTS)]
    hs = [x_ref[rows, :] + _dot(yc_ref[rows, :], w_out_c_ref[...]) + _dot(yh_ref[rows, :], w_out_h_ref[...])
          for rows in halves]
    pes = [_dot(p_ref[rows, :].astype(BF16), w_pp_ref[...]) for rows in halves]
    for rows, h, pe in zip(halves, hs, pes):
        gate = _sigmoid(_dot((_rms_scale(h) * pe_g_ref[...]).astype(BF16), w_pg_ref[...]))
        h = h + gate * pe
        out_ref[rows, :] = _rms_scale(h) * final_g_ref[...]


def _full(shape):
    return pl.BlockSpec(shape, lambda b, t: (0,) * len(shape))


def _layer(x, p_i, ln_g, w_in, conv_w, conv_b, cn_g, cn_b, w_pw2, b_pw2, lb_logits, onorm_g,
           w_out, pe_g, w_pg, w_pp, out_g, layer):
    batch, seq, d_model = x.shape
    width = conv_w.shape[-1]
    ple = p_i.shape[-1]
    tile = min(TIME_TILE, seq)
    assert seq % tile == 0 and tile % CHUNK == 0 and width % (2 * GROUP) == 0
    assert tile % (2 * SUBLANES * CONV_UNROLL) == 0 and width % CONV_LANES == 0
    grid = (batch, seq // tile)
    row2 = lambda a: a.reshape(1, -1)
    tok = lambda w: pl.BlockSpec((None, tile, w), lambda b, t: (b, t, 0))
    params = pltpu.CompilerParams(dimension_semantics=("arbitrary", "arbitrary"),
                                  vmem_limit_bytes=VMEM_LIMIT_BYTES)

    w_in_c = w_in[:, :3 * width].astype(BF16)
    w_in_h = w_in[:, 3 * width:].astype(BF16)

    conv_tile = min(CONV_TILE, seq)
    assert seq % conv_tile == 0 and conv_tile % (CONV_ROW_PARTS * SUBLANES * CONV_UNROLL) == 0
    ctok = lambda w: pl.BlockSpec((None, conv_tile, w), lambda b, t: (b, t, 0))
    y_conv = pl.pallas_call(
        functools.partial(_conv_branch_kernel, tile=conv_tile, width=width),
        grid=(batch, seq // conv_tile),
        in_specs=[ctok(d_model), _full((1, d_model)), _full((d_model, 3 * width)),
                  _full((CONV_K, width)), _full((1, width)), _full((1, width)), _full((1, width)),
                  _full((width, width)), _full((1, width))],
        out_specs=ctok(width),
        out_shape=jax.ShapeDtypeStruct((batch, seq, width), BF16),
        scratch_shapes=[pltpu.VMEM((HALO + conv_tile, width), F32), pltpu.VMEM((conv_tile, width), F32),
                        pltpu.VMEM((SUBLANES * CONV_K, width), F32)],
        compiler_params=params,
        name="conv_branch",
    )(x, row2(ln_g), w_in_c, conv_w, row2(conv_b), row2(cn_g), row2(cn_b),
      w_pw2.astype(BF16), row2(b_pw2))

    tril = np.tril(np.ones((CHUNK, CHUNK), np.float32))
    tri = jnp.asarray(np.concatenate([tril, tril], axis=1), BF16)
    y_hgrn = pl.pallas_call(
        functools.partial(_hgrn_branch_kernel, tile=tile, width=width, layer=layer),
        grid=grid,
        in_specs=[tok(d_model), _full((1, d_model)), _full((d_model, 4 * width)),
                  _full(lb_logits.shape), _full((1, width)), _full(tri.shape)],
        out_specs=tok(width),
        out_shape=jax.ShapeDtypeStruct((batch, seq, width), BF16),
        scratch_shapes=[pltpu.VMEM((width // GROUP, GROUP, GROUP), F32)],
        compiler_params=params,
        name="hgrn_branch",
    )(x, row2(ln_g), w_in_h, lb_logits, row2(onorm_g), tri)

    out_tile = min(OUTPUT_TILE, seq)
    assert seq % out_tile == 0 and out_tile % (OUTPUT_ROW_PARTS * SUBLANES) == 0
    otok = lambda w: pl.BlockSpec((None, out_tile, w), lambda b, t: (b, t, 0))
    return pl.pallas_call(
        _output_kernel,
        grid=(batch, seq // out_tile),
        in_specs=[otok(d_model), otok(width), otok(width), otok(ple),
                  _full((width, d_model)), _full((width, d_model)), _full((1, d_model)),
                  _full((d_model, d_model)), _full((ple, d_model)), _full((1, d_model))],
        out_specs=otok(d_model),
        out_shape=jax.ShapeDtypeStruct((batch, seq, d_model), x.dtype),
        compiler_params=params,
        name="output_proj",
    )(x, y_conv, y_hgrn, p_i, w_out[:width].astype(BF16), w_out[width:].astype(BF16),
      row2(pe_g), w_pg.astype(BF16), w_pp.astype(BF16), row2(out_g))


def kernel(x, p, ln_g, w_in, conv_w, conv_b, cnorm_g, cnorm_b, w_pw2, b_pw2, lb_logits, onorm_g,
           w_out, pe_norm_g, w_pg, w_pp, final_g):
    depth = p.shape[0]
    assert depth == 1, "the final norm is fused into the only layer's output kernel"
    return _layer(x, p[0], ln_g[0], w_in[0], conv_w[0], conv_b[0], cnorm_g[0], cnorm_b[0],
                  w_pw2[0], b_pw2[0], lb_logits, onorm_g[0], w_out[0], pe_norm_g[0], w_pg[0],
                  w_pp[0], final_g, layer=0)
```
